```python
import math, functools
import jax, jax.numpy as jnp
from jax import lax
import numpy as np

D_MODEL = 4096
BATCH = 1
SEQ = 8192
DEPTH = 2
DEC_BATCH = 1
DEC_SEQ = 16384
PAST_LEN = 128

HEAD_DIM = 128
N_MIX_HEADS = D_MODEL // HEAD_DIM
H_A = N_MIX_HEADS // 4
DILATED_PAIRS = ((128, 1), (512, 4), (2048, 16))
H_B_GROUP = (3 * N_MIX_HEADS // 8) // 3
H_B = H_B_GROUP * len(DILATED_PAIRS)
H_C = N_MIX_HEADS - H_A - H_B
KV_C = H_C // 3
MIX_WIDTH = (H_A + H_B + H_C) * HEAD_DIM
CONV_W = 4
CONV_PAD = ((CONV_W - 1) // 2, CONV_W // 2)
DN_CHUNK = 64
ROPE_THETA = 500000.0
ROT_DIMS = HEAD_DIM // 4
AXIAL_THETA = 10000.0
GRID_W = 64
Q_BLOCK = 128
N_MEM = 256
X_HEADS = 4
D_FF = 4 * D_MODEL
EPS = 1e-6
PROJ_SIZES = (3 * H_A * HEAD_DIM,
              H_A * HEAD_DIM,
              2 * H_A,
              2 * H_A,
              3 * H_B * HEAD_DIM,
              H_C * HEAD_DIM,
              KV_C * HEAD_DIM,
              KV_C * HEAD_DIM)
PROJ_COLS = sum(PROJ_SIZES)
PROJ_SPLITS = tuple(int(i) for i in np.cumsum(PROJ_SIZES)[:-1])

kernel_name = 'hybrid_bidir_deltanet_dilated_axial_encoder'

F32 = jnp.float32


def rmsnorm(x, g):
    xf = x.astype(F32)
    y = xf * lax.rsqrt(jnp.mean(xf * xf, axis=-1, keepdims=True) + EPS)
    return (y * g.astype(F32)).astype(x.dtype)


def l2norm(x):
    return x * lax.rsqrt(jnp.sum(x * x, axis=-1, keepdims=True) + EPS)


def rope(x, pos, theta):
    half = x.shape[-1] // 2
    inv = theta ** (-jnp.arange(half, dtype=F32) / half)
    ang = pos[:, None] * inv[None, :]
    cos = jnp.cos(ang)[:, None, :]
    sin = jnp.sin(ang)[:, None, :]
    x1 = x[..., :half].astype(F32)
    x2 = x[..., half:].astype(F32)
    return jnp.concatenate([x1 * cos - x2 * sin, x2 * cos + x1 * sin], axis=-1).astype(x.dtype)


def gated_delta_rule(q, k, v, beta, g):
    B, H, L, dk = q.shape
    dv = v.shape[-1]
    C = DN_CHUNK
    N = L // C
    q = q.reshape(B, H, N, C, dk)
    k = k.reshape(B, H, N, C, dk)
    v = v.reshape(B, H, N, C, dv)
    beta = beta.reshape(B, H, N, C)
    gc = jnp.cumsum(g.reshape(B, H, N, C), axis=-1)
    incl = jnp.tril(jnp.ones((C, C), bool))
    strict = jnp.tril(jnp.ones((C, C), bool), -1)
    decay = jnp.exp(jnp.where(incl, gc[..., :, None] - gc[..., None, :], -jnp.inf))
    kb = k * beta[..., None]
    vb = v * beta[..., None]
    a = jnp.where(strict, jnp.einsum('bhncd,bhnsd->bhncs', kb, k) * decay, 0.0) + jnp.eye(C, dtype=q.dtype)
    solve = functools.partial(lax.linalg.triangular_solve, left_side=True, lower=True, unit_diagonal=True)
    u = solve(a, vb)
    w = solve(a, kb * jnp.exp(gc)[..., None])
    qk = jnp.einsum('bhncd,bhnsd->bhncs', q, k) * decay

    def step(S, inp):
        qc, kc, uc, wc, gcc, qkc = inp
        v_new = uc - jnp.einsum('bhcd,bhde->bhce', wc, S)
        o = (jnp.einsum('bhcd,bhde->bhce', qc * jnp.exp(gcc)[..., None], S)
             + jnp.einsum('bhcs,bhse->bhce', qkc, v_new))
        glast = gcc[..., -1]
        S = (S * jnp.exp(glast)[..., None, None]
             + jnp.einsum('bhcd,bhce->bhde', kc * jnp.exp(glast[..., None] - gcc)[..., None], v_new))
        return S, o

    xs = tuple(jnp.moveaxis(t, 2, 0) for t in (q, k, u, w, gc, qk))
    S0 = jnp.zeros((B, H, dk, dv), q.dtype)
    _, o = lax.scan(step, S0, xs)
    return jnp.moveaxis(o, 0, 2).reshape(B, H, L, dv)


def deltanet_mixer(qkv, gate, beta_raw, decay_raw, conv_w, a_log, dt_bias, onorm_g):
    B, L, n_ch = qkv.shape
    qkv = lax.conv_general_dilated(qkv, conv_w[:, None, :].astype(qkv.dtype), window_strides=(1,),
                                   padding=(CONV_PAD,), dimension_numbers=('NWC', 'WIO', 'NWC'),
                                   feature_group_count=n_ch)
    qkv = jax.nn.silu(qkv.astype(F32)).reshape(B, L, 3, H_A, HEAD_DIM)
    q = l2norm(qkv[:, :, 0]) * HEAD_DIM ** -0.5
    k = l2norm(qkv[:, :, 1])
    v = qkv[:, :, 2]
    beta = jax.nn.sigmoid(beta_raw.astype(F32)).reshape(B, L, 2, H_A)
    g = -jnp.exp(a_log.astype(F32)) * jax.nn.softplus(decay_raw.astype(F32).reshape(B, L, 2, H_A)
                                                      + dt_bias.astype(F32))
    qh, kh, vh = (t.transpose(0, 2, 1, 3) for t in (q, k, v))
    b_f, b_b = beta[:, :, 0].transpose(0, 2, 1), beta[:, :, 1].transpose(0, 2, 1)
    g_f, g_b = g[:, :, 0].transpose(0, 2, 1), g[:, :, 1].transpose(0, 2, 1)
    o_fwd = gated_delta_rule(qh, kh, vh, b_f, g_f)
    flip = lambda t: jnp.flip(t, axis=2)
    o_bwd = flip(gated_delta_rule(flip(qh), flip(kh), flip(vh), flip(b_b), flip(g_b)))
    o = (o_fwd + o_bwd).transpose(0, 2, 1, 3)
    o = rmsnorm(o, onorm_g) * jax.nn.silu(gate.astype(F32).reshape(B, L, H_A, HEAD_DIM))
    return o.reshape(B, L, H_A * HEAD_DIM).astype(gate.dtype)


def dilated_window_attention(q, k, v, window, dilation):
    B, L, h, dh = q.shape
    d = dilation
    side = (window // 2) // d
    blk = side
    n = L // d
    nb = -(-n // blk)
    n_pad = nb * blk

    def to_sub(t):
        t = t.reshape(B, n, d, h, dh).transpose(0, 2, 3, 1, 4)
        return jnp.pad(t, ((0, 0), (0, 0), (0, 0), (0, n_pad - n), (0, 0)))

    def windows(t):
        tp = jnp.pad(t, ((0, 0), (0, 0), (0, 0), (blk, blk), (0, 0))).reshape(B, d, h, nb + 2, blk, dh)
        return jnp.concatenate([tp[:, :, :, :-2], tp[:, :, :, 1:-1], tp[:, :, :, 2:]], axis=4)

    qb = to_sub(q).reshape(B, d, h, nb, blk, dh)
    kw = windows(to_sub(k))
    vw = windows(to_sub(v)).astype(F32)
    qi = jnp.arange(n_pad).reshape(nb, blk)[:, :, None]
    ki = (jnp.arange(nb)[:, None] * blk - blk + jnp.arange(3 * blk)[None, :])[:, None, :]
    valid = (ki >= 0) & (ki < n) & (jnp.abs(qi - ki) <= side)
    s = jnp.einsum('brhcqe,brhcke->brhcqk', qb, kw).astype(F32) * dh ** -0.5
    s = jnp.where(valid, s, -jnp.inf)
    m = jnp.max(s, axis=-1, keepdims=True)
    e = jnp.exp(s - m)
    den = jnp.sum(e, axis=-1, keepdims=True)
    o = jnp.einsum('brhcqk,brhcke->brhcqe', e / den, vw)
    lse = (m + jnp.log(den))[..., 0]
    o = o.reshape(B, d, h, n_pad, dh)[:, :, :, :n].transpose(0, 3, 1, 2, 4).reshape(B, L, h, dh)
    lse = lse.reshape(B, d, h, n_pad)[..., :n].transpose(0, 3, 1, 2).reshape(B, L, h)
    return o, lse


def dilated_mixer(qkv):
    B, L, _ = qkv.shape
    qkv = qkv.reshape(B, L, 3, H_B, HEAD_DIM)
    pos = jnp.arange(L, dtype=F32)

    def partial_rope(t):
        return jnp.concatenate([rope(t[..., :ROT_DIMS], pos, ROPE_THETA), t[..., ROT_DIMS:]], axis=-1)

    q, k, v = partial_rope(qkv[:, :, 0]), partial_rope(qkv[:, :, 1]), qkv[:, :, 2]
    outs, lses = [], []
    for gi, (window, dilation) in enumerate(DILATED_PAIRS):
        hs = slice(gi * H_B_GROUP, (gi + 1) * H_B_GROUP)
        o, lse = dilated_window_attention(q[:, :, hs], k[:, :, hs], v[:, :, hs], window, dilation)
        outs.append(o)
        lses.append(lse)
    alpha = jax.nn.softmax(jnp.stack(lses), axis=0)
    mixed = jnp.concatenate([outs[gi] * alpha[gi][..., None] for gi in range(len(DILATED_PAIRS))], axis=2)
    return mixed.astype(qkv.dtype).reshape(B, L, H_B * HEAD_DIM)


def axial_gqa_mixer(q, k, v, qnorm_g, knorm_g):
    B, L, _ = q.shape
    q = rmsnorm(q.reshape(B, L, H_C, HEAD_DIM), qnorm_g)
    k = rmsnorm(k.reshape(B, L, KV_C, HEAD_DIM), knorm_g)
    v = v.reshape(B, L, KV_C, HEAD_DIM)
    rows = L // GRID_W
    row_pos = jnp.repeat(jnp.arange(rows, dtype=F32), GRID_W)
    col_pos = jnp.tile(jnp.arange(GRID_W, dtype=F32), rows)
    half = HEAD_DIM // 2

    def axial(t):
        return jnp.concatenate([rope(t[..., :half], row_pos, AXIAL_THETA),
                                rope(t[..., half:], col_pos, AXIAL_THETA)], axis=-1)

    q, k = axial(q), axial(k)
    G = H_C // KV_C
    nq = L // Q_BLOCK
    qb = q.reshape(B, nq, Q_BLOCK, KV_C, G, HEAD_DIM).transpose(1, 0, 2, 3, 4, 5)
    vf = v.astype(F32)

    def block(qc):
        s = jnp.einsum('bqkgd,bskd->bkgqs', qc, k).astype(F32) * HEAD_DIM ** -0.5
        p = jax.nn.softmax(s, axis=-1)
        return jnp.einsum('bkgqs,bskd->bqkgd', p, vf).astype(q.dtype)

    o = lax.map(block, qb)
    return o.transpose(1, 0, 2, 3, 4, 5).reshape(B, L, H_C * HEAD_DIM)


def memory_cross_attention(h, m, wq, wk, wv, wo):
    B, L, _ = h.shape
    M = m.shape[1]
    q = (h @ wq).reshape(B, L, X_HEADS, HEAD_DIM)
    k = (m @ wk).reshape(B, M, X_HEADS, HEAD_DIM)
    v = (m @ wv).reshape(B, M, X_HEADS, HEAD_DIM)
    s = jnp.einsum('blhd,bmhd->bhlm', q, k).astype(F32) * HEAD_DIM ** -0.5
    p = jax.nn.softmax(s, axis=-1)
    o = jnp.einsum('bhlm,bmhd->blhd', p, v.astype(F32)).astype(h.dtype)
    return o.reshape(B, L, X_HEADS * HEAD_DIM) @ wo


def encode(x, mem, norm_mix_g, w_in, conv_w, a_log, dt_bias, onorm_g, qnorm_g, knorm_g, w_out,
           norm_x_g, norm_mem_g, wq_x, wk_x, wv_x, wo_x, norm_ffn_g, w_up, w_down, norm_final_g):
    for l in range(DEPTH):
        h = rmsnorm(x, norm_mix_g[l])
        proj = h @ w_in[l]
        a_qkv, a_gate, a_beta, a_decay, b_qkv, c_q, c_k, c_v = jnp.split(proj, PROJ_SPLITS, axis=-1)
        y_a = deltanet_mixer(a_qkv, a_gate, a_beta, a_decay, conv_w[l], a_log[l], dt_bias[l], onorm_g[l])
        y_b = dilated_mixer(b_qkv)
        y_c = axial_gqa_mixer(c_q, c_k, c_v, qnorm_g[l], knorm_g[l])
        x = x + jnp.concatenate([y_a, y_b, y_c], axis=-1).astype(x.dtype) @ w_out[l]
        x = x + memory_cross_attention(rmsnorm(x, norm_x_g[l]), rmsnorm(mem, norm_mem_g[l]),
                                       wq_x[l], wk_x[l], wv_x[l], wo_x[l])
        hf = rmsnorm(x, norm_ffn_g[l])
        x = x + jnp.square(jax.nn.relu(hf @ w_up[l])) @ w_down[l]
    return rmsnorm(x, norm_final_g)


def setup_inputs(seed: int = 0) -> dict:
    key = jax.random.key(seed)
    ks = jax.random.split(key, 24)

    def w(k, shape, fan_in):
        return jax.random.normal(k, shape, F32) * fan_in ** -0.5

    def gain(k, shape):
        return 1.0 + 0.02 * jax.random.normal(k, shape, F32)

    dt = jnp.exp(jax.random.uniform(ks[8], (DEPTH, 2, H_A), F32, math.log(1e-3), math.log(1e-1)))
    return {
        'x_prompt': jax.random.normal(ks[0], (BATCH, SEQ, D_MODEL), F32),
        'x_sample': jax.random.normal(ks[1], (DEC_BATCH, DEC_SEQ, D_MODEL), F32),
        'mem_prompt': jax.random.normal(ks[2], (BATCH, N_MEM, D_MODEL), F32),
        'mem_sample': jax.random.normal(ks[3], (DEC_BATCH, N_MEM, D_MODEL), F32),
        'norm_mix_g': gain(ks[4], (DEPTH, D_MODEL)),
        'w_in': w(ks[5], (DEPTH, D_MODEL, PROJ_COLS), D_MODEL),
        'conv_w': w(ks[6], (DEPTH, CONV_W, 3 * H_A * HEAD_DIM), CONV_W),
        'a_log': jnp.log(jax.random.uniform(ks[7], (DEPTH, 2, H_A), F32, 1.0, 16.0)),
        'dt_bias': dt + jnp.log(-jnp.expm1(-dt)),
        'onorm_g': gain(ks[9], (DEPTH, HEAD_DIM)),
        'qnorm_g': gain(ks[10], (DEPTH, HEAD_DIM)),
        'knorm_g': gain(ks[11], (DEPTH, HEAD_DIM)),
        'w_out': w(ks[12], (DEPTH, MIX_WIDTH, D_MODEL), MIX_WIDTH),
        'norm_x_g': gain(ks[13], (DEPTH, D_MODEL)),
        'norm_mem_g': gain(ks[14], (DEPTH, D_MODEL)),
        'wq_x': w(ks[15], (DEPTH, D_MODEL, X_HEADS * HEAD_DIM), D_MODEL),
        'wk_x': w(ks[16], (DEPTH, D_MODEL, X_HEADS * HEAD_DIM), D_MODEL),
        'wv_x': w(ks[17], (DEPTH, D_MODEL, X_HEADS * HEAD_DIM), D_MODEL),
        'wo_x': w(ks[18], (DEPTH, X_HEADS * HEAD_DIM, D_MODEL), X_HEADS * HEAD_DIM),
        'norm_ffn_g': gain(ks[19], (DEPTH, D_MODEL)),
        'w_up': w(ks[20], (DEPTH, D_MODEL, D_FF), D_MODEL),
        'w_down': w(ks[21], (DEPTH, D_FF, D_MODEL), D_FF),
        'norm_final_g': gain(ks[22], (D_MODEL,)),
    }


def reference(x_prompt, x_sample, mem_prompt, mem_sample, norm_mix_g, w_in, conv_w, a_log, dt_bias,
              onorm_g, qnorm_g, knorm_g, w_out, norm_x_g, norm_mem_g, wq_x, wk_x, wv_x, wo_x,
              norm_ffn_g, w_up, w_down, norm_final_g):
    y_prompt = encode(x_prompt, mem_prompt, norm_mix_g, w_in, conv_w, a_log, dt_bias, onorm_g, qnorm_g,
                      knorm_g, w_out, norm_x_g, norm_mem_g, wq_x, wk_x, wv_x, wo_x, norm_ffn_g, w_up,
                      w_down, norm_final_g)
    y_sample = encode(x_sample, mem_sample, norm_mix_g, w_in, conv_w, a_log, dt_bias, onorm_g, qnorm_g,
                      knorm_g, w_out, norm_x_g, norm_mem_g, wq_x, wk_x, wv_x, wo_x, norm_ffn_g, w_up,
                      w_down, norm_final_g)
    return (y_prompt, y_sample)
```

```python
import functools
import math

import jax
import jax.numpy as jnp
from jax import lax
from jax.experimental import pallas as pl
from jax.experimental.pallas import tpu as pltpu

F32 = jnp.float32
BF16 = jnp.bfloat16

D_MODEL = 4096
HEAD_DIM = 128
H_A = 8
DILATED_PAIRS = ((128, 1), (512, 4), (2048, 16))
H_B_GROUP = 4
H_B = 12
H_C = 12
KV_C = 4
GQA_G = H_C // KV_C
CONV_W = 4
ROPE_THETA = 500000.0
ROT_DIMS = HEAD_DIM // 4
AXIAL_THETA = 10000.0
GRID_W = 64
N_MEM = 256
X_HEADS = 4
D_FF = 4 * D_MODEL
EPS = 1e-6
SCALE = HEAD_DIM ** -0.5
NEG_BIG = -1e30

W_A = H_A * HEAD_DIM
W_B = H_B * HEAD_DIM
W_CQ = H_C * HEAD_DIM
W_CKV = KV_C * HEAD_DIM
OFF_CQ = 0
OFF_BQ = OFF_CQ + W_CQ
OFF_BK = OFF_BQ + W_B
OFF_BV = OFF_BK + W_B
OFF_CK = OFF_BV + W_B
OFF_CV = OFF_CK + W_CKV
OFF_AQ = OFF_CV + W_CKV
OFF_AGATE = OFF_AQ + 3 * W_A
OFF_SMALL = OFF_AGATE + W_A
W_SMALL = 512
PROJ_PAD = OFF_SMALL + W_SMALL

V7X_VMEM_LIMIT_BYTES = 56 * 1024 * 1024
DN_CHUNK = 64


def _params(*sem):
    return pltpu.CompilerParams(dimension_semantics=sem, vmem_limit_bytes=V7X_VMEM_LIMIT_BYTES)


def _rmsnorm_kernel(x_ref, g_ref, o_ref):
    x = x_ref[...]
    ms = jnp.mean(x * x, axis=-1, keepdims=True)
    o_ref[...] = (x * lax.rsqrt(ms + EPS) * g_ref[...]).astype(o_ref.dtype)


def rmsnorm_rows(x, g, out_dtype, tm=256):
    m, d = x.shape
    return pl.pallas_call(
        _rmsnorm_kernel,
        grid=(m // tm,),
        in_specs=[pl.BlockSpec((tm, d), lambda i: (i, 0)), pl.BlockSpec((1, d), lambda i: (0, 0))],
        out_specs=pl.BlockSpec((tm, d), lambda i: (i, 0)),
        out_shape=jax.ShapeDtypeStruct((m, d), out_dtype),
        name="rmsnorm_rows",
        compiler_params=_params("parallel"),
    )(x, g.reshape(1, d))


def _mm_kernel(*refs, nk, relu2, has_res, use_scratch):
    a_ref, w_ref = refs[0], refs[1]
    r_ref = refs[2] if has_res else None
    o_ref = refs[3] if has_res else refs[2]
    part = jnp.dot(a_ref[...], w_ref[...], preferred_element_type=F32)

    def finish(acc):
        if relu2:
            acc = jnp.square(jnp.maximum(acc, 0.0))
        return acc.astype(o_ref.dtype)

    if nk == 1:
        if has_res:
            part = r_ref[...] + part
        o_ref[...] = finish(part)
        return
    acc_ref = refs[-1] if use_scratch else o_ref
    k = pl.program_id(2)

    @pl.when(k == 0)
    def _():
        acc_ref[...] = (r_ref[...] + part) if has_res else part

    @pl.when(k > 0)
    def _():
        acc_ref[...] += part

    if use_scratch or relu2:
        @pl.when(k == nk - 1)
        def _():
            o_ref[...] = finish(acc_ref[...])


def matmul(a, w, *, tm, tn, tk=None, residual=None, relu2=False, out_dtype=F32):
    m, kd = a.shape
    n = w.shape[1]
    tm = min(tm, m)
    tk = kd if tk is None else tk
    nk = kd // tk
    has_res = residual is not None
    use_scratch = nk > 1 and out_dtype != F32
    in_specs = [pl.BlockSpec((tm, tk), lambda i, j, k: (i, k)),
                pl.BlockSpec((tk, tn), lambda i, j, k: (k, j))]
    args = [a, w]
    if has_res:
        in_specs.append(pl.BlockSpec((tm, tn), lambda i, j, k: (i, j)))
        args.append(residual)
    return pl.pallas_call(
        functools.partial(_mm_kernel, nk=nk, relu2=relu2, has_res=has_res, use_scratch=use_scratch),
        grid=(m // tm, n // tn, nk),
        in_specs=in_specs,
        out_specs=pl.BlockSpec((tm, tn), lambda i, j, k: (i, j)),
        out_shape=jax.ShapeDtypeStruct((m, n), out_dtype),
        scratch_shapes=[pltpu.VMEM((tm, tn), F32)] if use_scratch else [],
        name="matmul",
        compiler_params=_params("parallel", "parallel", "arbitrary"),
    )(*args)


def _rope_tables(pos_list, theta, width_each):
    half = width_each // 2
    inv = theta ** (-jnp.arange(half, dtype=F32) / half)
    cos_parts, sin_parts = [], []
    for pos in pos_list:
        ang = pos[:, None] * inv[None, :]
        c, s = jnp.cos(ang), jnp.sin(ang)
        cos_parts += [c, c]
        sin_parts += [-s, s]
    length = pos_list[0].shape[0]
    rest = HEAD_DIM - width_each * len(pos_list)
    if rest:
        cos_parts.append(jnp.ones((length, rest), F32))
        sin_parts.append(jnp.zeros((length, rest), F32))
    return jnp.concatenate(cos_parts, axis=1), jnp.concatenate(sin_parts, axis=1)


def _head_prep_kernel(*refs, n_heads, half, use_norm, use_rope, scale):
    x_ref = refs[0]
    o_ref = refs[-1]
    idx = 1
    if use_norm:
        g = refs[idx][...]
        idx += 1
    if use_rope:
        cos = refs[idx][...]
        sin = refs[idx + 1][...]
        lane = lax.broadcasted_iota(jnp.int32, cos.shape, 1)
        first = (lane % (2 * half)) < half
    for h in range(n_heads):
        x = x_ref[:, h * HEAD_DIM:(h + 1) * HEAD_DIM]
        if use_norm:
            ms = jnp.mean(x * x, axis=-1, keepdims=True)
            x = x * lax.rsqrt(ms + EPS) * g
        if use_rope:
            partner = jnp.where(first, pltpu.roll(x, HEAD_DIM - half, axis=1), pltpu.roll(x, half, axis=1))
            x = x * cos + partner * sin
        if scale != 1.0:
            x = x * scale
        o_ref[h] = x.astype(o_ref.dtype)


def head_prep(proj, col_off, n_heads, *, gain=None, tables=None, half=None, scale=1.0, tm=256):
    length = proj.shape[0]
    width = n_heads * HEAD_DIM
    in_specs = [pl.BlockSpec((tm, width), lambda i: (i, col_off // width))]
    args = [proj]
    if gain is not None:
        in_specs.append(pl.BlockSpec((1, HEAD_DIM), lambda i: (0, 0)))
        args.append(gain.reshape(1, HEAD_DIM))
    if tables is not None:
        in_specs += [pl.BlockSpec((tm, HEAD_DIM), lambda i: (i, 0))] * 2
        args += list(tables)
    return pl.pallas_call(
        functools.partial(_head_prep_kernel, n_heads=n_heads, half=half, use_norm=gain is not None,
                          use_rope=tables is not None, scale=scale),
        grid=(length // tm,),
        in_specs=in_specs,
        out_specs=pl.BlockSpec((n_heads, tm, HEAD_DIM), lambda i: (0, i, 0)),
        out_shape=jax.ShapeDtypeStruct((n_heads, length, HEAD_DIM), BF16),
        name="head_prep",
        compiler_params=_params("parallel"),
    )(*args)


def _flash_kernel(q_ref, k_ref, v_ref, o_ref, m_ref, l_ref, acc_ref, *, tq, nkv):
    ki = pl.program_id(2)

    @pl.when(ki == 0)
    def _():
        m_ref[...] = jnp.full(m_ref.shape, NEG_BIG, F32)
        l_ref[...] = jnp.zeros(l_ref.shape, F32)
        acc_ref[...] = jnp.zeros(acc_ref.shape, F32)

    q = q_ref[...].reshape(GQA_G * tq, HEAD_DIM)
    s = lax.dot_general(q, k_ref[0], (((1,), (1,)), ((), ())), preferred_element_type=F32)
    m_prev = m_ref[...]
    m_new = jnp.maximum(m_prev, jnp.max(s, axis=-1, keepdims=True))
    alpha = jnp.exp(m_prev - m_new)
    p = jnp.exp(s - m_new)
    l_ref[...] = alpha * l_ref[...] + jnp.sum(p, axis=-1, keepdims=True)
    acc_ref[...] = alpha * acc_ref[...] + jnp.dot(p.astype(BF16), v_ref[0], preferred_element_type=F32)
    m_ref[...] = m_new

    @pl.when(ki == nkv - 1)
    def _():
        o = acc_ref[...] / l_ref[...]
        for g in range(GQA_G):
            o_ref[:, g * HEAD_DIM:(g + 1) * HEAD_DIM] = o[g * tq:(g + 1) * tq].astype(o_ref.dtype)


def axial_flash_attention(q, k, v, *, tq=256, tk=512):
    length = q.shape[1]
    nkv = length // tk
    rows = GQA_G * tq
    return pl.pallas_call(
        functools.partial(_flash_kernel, tq=tq, nkv=nkv),
        grid=(KV_C, length // tq, nkv),
        in_specs=[pl.BlockSpec((GQA_G, tq, HEAD_DIM), lambda h, i, j: (h, i, 0)),
                  pl.BlockSpec((1, tk, HEAD_DIM), lambda h, i, j: (h, j, 0)),
                  pl.BlockSpec((1, tk, HEAD_DIM), lambda h, i, j: (h, j, 0))],
        out_specs=pl.BlockSpec((tq, GQA_G * HEAD_DIM), lambda h, i, j: (i, h)),
        out_shape=jax.ShapeDtypeStruct((length, H_C * HEAD_DIM), BF16),
        scratch_shapes=[pltpu.VMEM((rows, 1), F32), pltpu.VMEM((rows, 1), F32),
                        pltpu.VMEM((rows, HEAD_DIM), F32)],
        name="axial_flash",
        compiler_params=_params("parallel", "parallel", "arbitrary"),
    )(q, k, v)


def _dilated_kernel(q_ref, kp_ref, kc_ref, kn_ref, vp_ref, vc_ref, vn_ref, o_ref, lse_ref, *, tq, side, n_sub):
    c = pl.program_id(1)
    row = lax.broadcasted_iota(jnp.int32, (tq, tq + 2 * side), 0)
    col = lax.broadcasted_iota(jnp.int32, (tq, tq + 2 * side), 1)
    key_idx = c * tq - side + col
    valid = (jnp.abs(row + side - col) <= side) & (key_idx >= 0) & (key_idx < n_sub)
    for h in range(H_B_GROUP):
        kk = jnp.concatenate([kp_ref[h, tq - side:, :], kc_ref[h], kn_ref[h, :side, :]], axis=0)
        vv = jnp.concatenate([vp_ref[h, tq - side:, :], vc_ref[h], vn_ref[h, :side, :]], axis=0)
        s = lax.dot_general(q_ref[h], kk, (((1,), (1,)), ((), ())), preferred_element_type=F32)
        s = jnp.where(valid, s, NEG_BIG)
        m = jnp.max(s, axis=-1, keepdims=True)
        e = jnp.where(valid, jnp.exp(s - m), 0.0)
        den = jnp.sum(e, axis=-1, keepdims=True)
        o = jnp.dot(e.astype(BF16), vv, preferred_element_type=F32) / den
        o_ref[h] = o
        lse_ref[h] = jnp.broadcast_to(m + jnp.log(den), (tq, HEAD_DIM))


def dilated_group_attention(q, k, v, group, window, dilation, *, tq=256):
    length = q.shape[1]
    d = dilation
    side = (window // 2) // d
    n_sub = length // d
    nblk = n_sub // tq
    qv, kv, vv = (t.reshape(H_B, n_sub, d * HEAD_DIM) for t in (q, k, v))
    cur = lambda r, c: (group, c, r)
    prev = lambda r, c: (group, jnp.maximum(c - 1, 0), r)
    nxt = lambda r, c: (group, jnp.minimum(c + 1, nblk - 1), r)
    blk = (H_B_GROUP, tq, HEAD_DIM)
    out, lse = pl.pallas_call(
        functools.partial(_dilated_kernel, tq=tq, side=side, n_sub=n_sub),
        grid=(d, nblk),
        in_specs=[pl.BlockSpec(blk, cur), pl.BlockSpec(blk, prev), pl.BlockSpec(blk, cur), pl.BlockSpec(blk, nxt),
                  pl.BlockSpec(blk, prev), pl.BlockSpec(blk, cur), pl.BlockSpec(blk, nxt)],
        out_specs=[pl.BlockSpec(blk, lambda r, c: (0, c, r))] * 2,
        out_shape=[jax.ShapeDtypeStruct((H_B_GROUP, n_sub, d * HEAD_DIM), F32)] * 2,
        name="dilated_attn",
        compiler_params=_params("parallel", "parallel"),
    )(qv, kv, kv, kv, vv, vv, vv)
    return out.reshape(H_B_GROUP, length, HEAD_DIM), lse.reshape(H_B_GROUP, length, HEAD_DIM)


def _conv_prep_kernel(cur_ref, prev_ref, next_ref, w_ref, o_ref, *, tm, nblk):
    i = pl.program_id(0)
    j = pl.program_id(1)
    x = cur_ref[...]
    before = jnp.where(i > 0, prev_ref[7:8, :], 0.0)
    after = jnp.where(i < nblk - 1, next_ref[0:2, :], 0.0)
    row = lax.broadcasted_iota(jnp.int32, x.shape, 0)
    xm1 = jnp.where(row == 0, before, pltpu.roll(x, 1, axis=0))
    xp1 = jnp.where(row == tm - 1, after[0:1, :], pltpu.roll(x, tm - 1, axis=0))
    xp2 = jnp.where(row == tm - 2, after[0:1, :],
                    jnp.where(row == tm - 1, after[1:2, :], pltpu.roll(x, tm - 2, axis=0)))
    w = w_ref[...]
    y = xm1 * w[0:1, :] + x * w[1:2, :] + xp1 * w[2:3, :] + xp2 * w[3:4, :]
    y = y * jax.nn.sigmoid(y)
    post = jnp.where(j == 0, SCALE, 1.0)
    for h in range(H_A):
        yh = y[:, h * HEAD_DIM:(h + 1) * HEAD_DIM]
        inv = lax.rsqrt(jnp.sum(yh * yh, axis=-1, keepdims=True) + EPS) * post
        o_ref[:, h * HEAD_DIM:(h + 1) * HEAD_DIM] = yh * jnp.where(j < 2, inv, 1.0)


def deltanet_conv_prep(proj, conv_w, *, tm=256):
    length = proj.shape[0]
    nblk = length // tm
    cb = OFF_AQ // W_A
    hb = tm // 8
    return pl.pallas_call(
        functools.partial(_conv_prep_kernel, tm=tm, nblk=nblk),
        grid=(nblk, 3),
        in_specs=[pl.BlockSpec((tm, W_A), lambda i, j: (i, cb + j)),
                  pl.BlockSpec((8, W_A), lambda i, j: (jnp.maximum(i * hb - 1, 0), cb + j)),
                  pl.BlockSpec((8, W_A), lambda i, j: (jnp.minimum((i + 1) * hb, nblk * hb - 1), cb + j)),
                  pl.BlockSpec((CONV_W, W_A), lambda i, j: (0, j))],
        out_specs=pl.BlockSpec((tm, W_A), lambda i, j: (i, j)),
        out_shape=jax.ShapeDtypeStruct((length, 3 * W_A), F32),
        name="delta_conv_prep",
        compiler_params=_params("parallel", "parallel"),
    )(proj, proj, proj, conv_w)


def _gate_prep_kernel(x_ref, a_ref, dt_ref, o_ref):
    x = x_ref[...]
    lane = lax.broadcasted_iota(jnp.int32, x.shape, 1)
    beta = jax.nn.sigmoid(x)
    z = x + dt_ref[...]
    softplus = jnp.maximum(z, 0.0) + jnp.log(1.0 + jnp.exp(-jnp.abs(z)))
    g = -jnp.exp(a_ref[...]) * softplus
    o_ref[...] = jnp.where(lane < 2 * H_A, beta, jnp.where(lane < 4 * H_A, g, 0.0))


def deltanet_gate_prep(proj, a_log, dt_bias, *, tm=512):
    length = proj.shape[0]
    pad = lambda t: jnp.zeros((1, HEAD_DIM), F32).at[0, 2 * H_A:4 * H_A].set(t.reshape(-1))
    return pl.pallas_call(
        _gate_prep_kernel,
        grid=(length // tm,),
        in_specs=[pl.BlockSpec((tm, HEAD_DIM), lambda i: (i, OFF_SMALL // HEAD_DIM)),
                  pl.BlockSpec((1, HEAD_DIM), lambda i: (0, 0)),
                  pl.BlockSpec((1, HEAD_DIM), lambda i: (0, 0))],
        out_specs=pl.BlockSpec((tm, HEAD_DIM), lambda i: (i, 0)),
        out_shape=jax.ShapeDtypeStruct((length, HEAD_DIM), F32),
        name="delta_gate_prep",
        compiler_params=_params("parallel"),
    )(proj, pad(a_log), pad(dt_bias))


def _bdot(a, b):
    return jnp.dot(a.astype(BF16), b.astype(BF16), preferred_element_type=F32)


def _split_dot(tri, x):
    hi = x.astype(BF16)
    lo = (x - hi.astype(F32)).astype(BF16)
    return jnp.dot(tri, hi, preferred_element_type=F32) + jnp.dot(tri, lo, preferred_element_type=F32)


def _delta_direction(direction, q_ref, k_ref, v_ref, gb_ref, o_ref, s_ref, chunk):
    cc = chunk
    row = lax.broadcasted_iota(jnp.int32, (cc, cc), 0)
    col = lax.broadcasted_iota(jnp.int32, (cc, cc), 1)
    incl = (row >= col) if direction == 0 else (row <= col)
    strict = (row > col) if direction == 0 else (row < col)
    last = cc - 1 if direction == 0 else 0
    gb = gb_ref[...]
    gc_all = _split_dot(incl.astype(BF16), gb)
    gc_rows = jnp.concatenate([gc_all, jnp.zeros_like(gc_all)], axis=0).T if cc < HEAD_DIM else gc_all.T
    eye = (row == col).astype(F32)
    level_masks = []
    differ = row ^ col
    b = 2
    while b <= cc:
        level_masks.append((differ < b) & (differ >= b // 2))
        b *= 2
    for h in range(H_A):
        lane_beta = direction * H_A + h
        lane_g = 2 * H_A + direction * H_A + h
        sl = slice(h * HEAD_DIM, (h + 1) * HEAD_DIM)
        q, k, v = q_ref[:, sl], k_ref[:, sl], v_ref[:, sl]
        beta = gb[:, lane_beta:lane_beta + 1]
        gcc = gc_all[:, lane_g:lane_g + 1]
        gcr = gc_rows[lane_g:lane_g + 1, 0:cc]
        decay = jnp.exp(jnp.where(incl, gcc - gcr, NEG_BIG))
        egc = jnp.exp(gcc)
        kb = k * beta
        vb = v * beta
        kk = lax.dot_general(jnp.concatenate([kb, q], axis=0).astype(BF16), k.astype(BF16),
                             (((1,), (1,)), ((), ())), preferred_element_type=F32)
        a = jnp.where(strict, kk[:cc] * decay, 0.0)
        qk = kk[cc:] * decay
        t = eye - jnp.where(level_masks[0], a, 0.0)
        for mask in level_masks[1:]:
            t = t - _bdot(_bdot(t, jnp.where(mask, a, 0.0)), t)
        uw = _bdot(t, jnp.concatenate([vb, kb * egc], axis=1))
        u, w = uw[:, :HEAD_DIM], uw[:, HEAD_DIM:]
        s_old = s_ref[direction * H_A + h]
        wq = _bdot(jnp.concatenate([w, q * egc], axis=0), s_old)
        v_new = u - wq[:cc]
        o_ref[:, sl] = wq[cc:] + _bdot(qk, v_new)
        g_last = gcc[last:last + 1, :]
        kg = k * jnp.exp(g_last - gcc)
        s_ref[direction * H_A + h] = s_old * jnp.exp(g_last) + lax.dot_general(
            kg.astype(BF16), v_new.astype(BF16), (((0,), (0,)), ((), ())), preferred_element_type=F32)


def _delta_kernel(qf, kf, vf, gf, qb, kb, vb, gbk, of_ref, ob_ref, s_ref, *, chunk):
    @pl.when(pl.program_id(0) == 0)
    def _():
        s_ref[...] = jnp.zeros(s_ref.shape, F32)

    _delta_direction(0, qf, kf, vf, gf, of_ref, s_ref, chunk)
    _delta_direction(1, qb, kb, vb, gbk, ob_ref, s_ref, chunk)


def deltanet_recurrence(qkv, gb, *, chunk=DN_CHUNK):
    length = qkv.shape[0]
    n = length // chunk
    fwd = lambda j: (lambda c: (c, j))
    bwd = lambda j: (lambda c: (n - 1 - c, j))
    col = lambda m, width: pl.BlockSpec((chunk, width), m)
    return pl.pallas_call(
        functools.partial(_delta_kernel, chunk=chunk),
        grid=(n,),
        in_specs=[col(fwd(0), W_A), col(fwd(1), W_A), col(fwd(2), W_A), col(fwd(0), HEAD_DIM),
                  col(bwd(0), W_A), col(bwd(1), W_A), col(bwd(2), W_A), col(bwd(0), HEAD_DIM)],
        out_specs=[col(fwd(0), W_A), col(bwd(0), W_A)],
        out_shape=[jax.ShapeDtypeStruct((length, W_A), F32)] * 2,
        scratch_shapes=[pltpu.VMEM((2 * H_A, HEAD_DIM, HEAD_DIM), F32)],
        name="delta_recurrence",
        compiler_params=_params("arbitrary"),
    )(qkv, qkv, qkv, gb, qkv, qkv, qkv, gb)


def _assemble_kernel(of_ref, ob_ref, gate_ref, og_ref, o0, o1, o2, l0, l1, l2, yc_ref, mix_ref):
    og = og_ref[...]
    for h in range(H_A):
        sl = slice(h * HEAD_DIM, (h + 1) * HEAD_DIM)
        o = of_ref[:, sl] + ob_ref[:, sl]
        ms = jnp.mean(o * o, axis=-1, keepdims=True)
        gate = gate_ref[:, sl]
        mix_ref[:, sl] = (o * lax.rsqrt(ms + EPS) * og * (gate * jax.nn.sigmoid(gate))).astype(mix_ref.dtype)
    outs, lses = (o0, o1, o2), (l0, l1, l2)
    for h in range(H_B_GROUP):
        ls = [r[h] for r in lses]
        mx = jnp.maximum(jnp.maximum(ls[0], ls[1]), ls[2])
        es = [jnp.exp(t - mx) for t in ls]
        tot = es[0] + es[1] + es[2]
        for gi in range(len(DILATED_PAIRS)):
            c0 = W_A + (gi * H_B_GROUP + h) * HEAD_DIM
            mix_ref[:, c0:c0 + HEAD_DIM] = (outs[gi][h] * (es[gi] / tot)).astype(mix_ref.dtype)
    mix_ref[:, W_A + W_B:] = yc_ref[...]


def assemble_mix(o_fwd, o_bwd, proj, onorm_g, dil_outs, dil_lses, y_c, *, tm=256):
    length = o_fwd.shape[0]
    row = lambda w: pl.BlockSpec((tm, w), lambda i: (i, 0))
    hblk = pl.BlockSpec((H_B_GROUP, tm, HEAD_DIM), lambda i: (0, i, 0))
    return pl.pallas_call(
        _assemble_kernel,
        grid=(length // tm,),
        in_specs=[row(W_A), row(W_A), pl.BlockSpec((tm, W_A), lambda i: (i, OFF_AGATE // W_A)),
                  pl.BlockSpec((1, HEAD_DIM), lambda i: (0, 0))] + [hblk] * 6 + [row(W_CQ)],
        out_specs=row(D_MODEL),
        out_shape=jax.ShapeDtypeStruct((length, D_MODEL), BF16),
        name="assemble_mix",
        compiler_params=_params("parallel"),
    )(o_fwd, o_bwd, proj, onorm_g.reshape(1, HEAD_DIM), *dil_outs, *dil_lses, y_c)


def _cross_kernel(x_ref, gx_ref, wq_ref, k_ref, v_ref, wo_ref, gf_ref, xo_ref, hf_ref):
    x = x_ref[...]
    ms = jnp.mean(x * x, axis=-1, keepdims=True)
    h = (x * lax.rsqrt(ms + EPS) * gx_ref[...]).astype(BF16)
    q = jnp.dot(h, wq_ref[...], preferred_element_type=F32)
    heads = []
    for hh in range(X_HEADS):
        sl = slice(hh * HEAD_DIM, (hh + 1) * HEAD_DIM)
        s = lax.dot_general(q[:, sl].astype(BF16), k_ref[:, sl], (((1,), (1,)), ((), ())),
                            preferred_element_type=F32) * SCALE
        m = jnp.max(s, axis=-1, keepdims=True)
        e = jnp.exp(s - m)
        p = e / jnp.sum(e, axis=-1, keepdims=True)
        heads.append(jnp.dot(p.astype(BF16), v_ref[:, sl], preferred_element_type=F32).astype(BF16))
    o = jnp.concatenate(heads, axis=1)
    xn = x + jnp.dot(o, wo_ref[...], preferred_element_type=F32)
    xo_ref[...] = xn
    ms2 = jnp.mean(xn * xn, axis=-1, keepdims=True)
    hf_ref[...] = (xn * lax.rsqrt(ms2 + EPS) * gf_ref[...]).astype(hf_ref.dtype)


def cross_attention_block(x, g_x, wq, kv, wo, g_ffn, *, tm=256):
    length = x.shape[0]
    width = X_HEADS * HEAD_DIM
    full = lambda shape: pl.BlockSpec(shape, lambda i: (0, 0))
    row = pl.BlockSpec((tm, D_MODEL), lambda i: (i, 0))
    return pl.pallas_call(
        _cross_kernel,
        grid=(length // tm,),
        in_specs=[row, full((1, D_MODEL)), full((D_MODEL, width)),
                  pl.BlockSpec((N_MEM, width), lambda i: (0, 0)), pl.BlockSpec((N_MEM, width), lambda i: (0, 1)),
                  full((width, D_MODEL)), full((1, D_MODEL))],
        out_specs=[row, row],
        out_shape=[jax.ShapeDtypeStruct((length, D_MODEL), F32), jax.ShapeDtypeStruct((length, D_MODEL), BF16)],
        name="cross_attn",
        compiler_params=_params("parallel"),
    )(x, g_x.reshape(1, D_MODEL), wq, kv, kv, wo, g_ffn.reshape(1, D_MODEL))


def _permute_w_in(w):
    sizes = (3 * W_A, W_A, 2 * H_A, 2 * H_A, 3 * W_B, W_CQ, W_CKV, W_CKV)
    offs = [0]
    for s in sizes:
        offs.append(offs[-1] + s)
    a_qkv, a_gate, a_beta, a_decay, b_qkv, c_q, c_k, c_v = (w[:, offs[i]:offs[i + 1]] for i in range(8))
    pad = jnp.zeros((w.shape[0], W_SMALL - 4 * H_A), w.dtype)
    return jnp.concatenate([c_q, b_qkv, c_k, c_v, a_qkv, a_gate, a_beta, a_decay, pad], axis=1).astype(BF16)


def _encode(x, mem, tables_b, tables_c, p):
    for l in range(len(p["w_in"])):
        h = rmsnorm_rows(x, p["norm_mix_g"][l], BF16)
        proj = matmul(h, p["w_in"][l], tm=1024, tn=512)
        qkv_a = deltanet_conv_prep(proj, p["conv_w"][l])
        gb = deltanet_gate_prep(proj, p["a_log"][l], p["dt_bias"][l])
        o_fwd, o_bwd = deltanet_recurrence(qkv_a, gb)
        q_b = head_prep(proj, OFF_BQ, H_B, tables=tables_b, half=ROT_DIMS // 2, scale=SCALE)
        k_b = head_prep(proj, OFF_BK, H_B, tables=tables_b, half=ROT_DIMS // 2)
        v_b = head_prep(proj, OFF_BV, H_B)
        dil = [dilated_group_attention(q_b, k_b, v_b, gi, w, d) for gi, (w, d) in enumerate(DILATED_PAIRS)]
        q_c = head_prep(proj, OFF_CQ, H_C, gain=p["qnorm_g"][l], tables=tables_c, half=HEAD_DIM // 4, scale=SCALE)
        k_c = head_prep(proj, OFF_CK, KV_C, gain=p["knorm_g"][l], tables=tables_c, half=HEAD_DIM // 4)
        v_c = head_prep(proj, OFF_CV, KV_C)
        y_c = axial_flash_attention(q_c, k_c, v_c)
        mix = assemble_mix(o_fwd, o_bwd, proj, p["onorm_g"][l], [t[0] for t in dil], [t[1] for t in dil], y_c)
        x = matmul(mix, p["w_out"][l], tm=1024, tn=512, residual=x)
        mem_n = rmsnorm_rows(mem, p["norm_mem_g"][l], BF16)
        kv = matmul(mem_n, p["wkv_x"][l], tm=N_MEM, tn=512, out_dtype=BF16)
        x, hf = cross_attention_block(x, p["norm_x_g"][l], p["wq_x"][l], kv, p["wo_x"][l], p["norm_ffn_g"][l])
        up = matmul(hf, p["w_up"][l], tm=1024, tn=1024, relu2=True, out_dtype=BF16)
        x = matmul(up, p["w_down"][l], tm=1024, tn=1024, tk=2048, residual=x)
    return rmsnorm_rows(x, p["norm_final_g"], F32)


def _tables(length):
    pos = jnp.arange(length, dtype=F32)
    tables_b = _rope_tables([pos], ROPE_THETA, ROT_DIMS)
    rows = length // GRID_W
    row_pos = jnp.repeat(jnp.arange(rows, dtype=F32), GRID_W)
    col_pos = jnp.tile(jnp.arange(GRID_W, dtype=F32), rows)
    tables_c = _rope_tables([row_pos, col_pos], AXIAL_THETA, HEAD_DIM // 2)
    return tables_b, tables_c


def kernel(x_prompt, x_sample, mem_prompt, mem_sample, norm_mix_g, w_in, conv_w, a_log, dt_bias, onorm_g, qnorm_g, knorm_g, w_out, norm_x_g, norm_mem_g, wq_x, wk_x, wv_x, wo_x, norm_ffn_g, w_up, w_down, norm_final_g):
    depth = w_in.shape[0]
    p = dict(
        norm_mix_g=norm_mix_g, conv_w=conv_w, a_log=a_log, dt_bias=dt_bias, onorm_g=onorm_g, qnorm_g=qnorm_g,
        knorm_g=knorm_g, norm_x_g=norm_x_g, norm_mem_g=norm_mem_g, norm_ffn_g=norm_ffn_g, norm_final_g=norm_final_g,
        w_in=[_permute_w_in(w_in[l]) for l in range(depth)],
        w_out=w_out.astype(BF16), wq_x=wq_x.astype(BF16),
        wkv_x=jnp.concatenate([wk_x, wv_x], axis=-1).astype(BF16),
        wo_x=wo_x.astype(BF16), w_up=w_up.astype(BF16), w_down=w_down.astype(BF16),
    )
    outs = []
    for x, mem in ((x_prompt, mem_prompt), (x_sample, mem_sample)):
        length = x.shape[1]
        tables_b, tables_c = _tables(length)
        y = _encode(x[0], mem[0], tables_b, tables_c, p)
        outs.append(y[None])
    return tuple(outs)
```

```python
import functools
import math

import jax
import jax.numpy as jnp
from jax import lax
from jax.experimental import pallas as pl
from jax.experimental.pallas import tpu as pltpu

F32 = jnp.float32
BF16 = jnp.bfloat16

D_MODEL = 4096
HEAD_DIM = 128
H_A = 8
DILATED_PAIRS = ((128, 1), (512, 4), (2048, 16))
H_B_GROUP = 4
H_B = 12
H_C = 12
KV_C = 4
GQA_G = H_C // KV_C
CONV_W = 4
ROPE_THETA = 500000.0
ROT_DIMS = HEAD_DIM // 4
AXIAL_THETA = 10000.0
GRID_W = 64
N_MEM = 256
X_HEADS = 4
D_FF = 4 * D_MODEL
EPS = 1e-6
SCALE = HEAD_DIM ** -0.5
LOG2_E = math.log2(math.e)
NEG_BIG = -1e30

W_A = H_A * HEAD_DIM
W_B = H_B * HEAD_DIM
W_CQ = H_C * HEAD_DIM
W_CKV = KV_C * HEAD_DIM
OFF_CQ = 0
OFF_BQ = OFF_CQ + W_CQ
OFF_BK = OFF_BQ + W_B
OFF_BV = OFF_BK + W_B
OFF_CK = OFF_BV + W_B
OFF_CV = OFF_CK + W_CKV
OFF_AQ = OFF_CV + W_CKV
OFF_AGATE = OFF_AQ + 3 * W_A
OFF_SMALL = OFF_AGATE + W_A
W_SMALL = 512
PROJ_PAD = OFF_SMALL + W_SMALL

V7X_VMEM_LIMIT_BYTES = 56 * 1024 * 1024
DN_CHUNK = 64


def _params(*sem):
    return pltpu.CompilerParams(dimension_semantics=sem, vmem_limit_bytes=V7X_VMEM_LIMIT_BYTES)


def _rmsnorm_kernel(x_ref, g_ref, o_ref):
    x = x_ref[...]
    ms = jnp.mean(x * x, axis=-1, keepdims=True)
    o_ref[...] = (x * lax.rsqrt(ms + EPS) * g_ref[...]).astype(o_ref.dtype)


def rmsnorm_rows(x, g, out_dtype, tm=256):
    m, d = x.shape
    return pl.pallas_call(
        _rmsnorm_kernel,
        grid=(m // tm,),
        in_specs=[pl.BlockSpec((tm, d), lambda i: (i, 0)), pl.BlockSpec((1, d), lambda i: (0, 0))],
        out_specs=pl.BlockSpec((tm, d), lambda i: (i, 0)),
        out_shape=jax.ShapeDtypeStruct((m, d), out_dtype),
        name="rmsnorm_rows",
        compiler_params=_params("parallel"),
    )(x, g.reshape(1, d))


def _mm_kernel(*refs, nk, relu2, has_res, use_scratch):
    a_ref, w_ref = refs[0], refs[1]
    r_ref = refs[2] if has_res else None
    o_ref = refs[3] if has_res else refs[2]
    part = jnp.dot(a_ref[...], w_ref[...], preferred_element_type=F32)

    def finish(acc):
        if relu2:
            acc = jnp.square(jnp.maximum(acc, 0.0))
        return acc.astype(o_ref.dtype)

    if nk == 1:
        if has_res:
            part = r_ref[...] + part
        o_ref[...] = finish(part)
        return
    acc_ref = refs[-1] if use_scratch else o_ref
    k = pl.program_id(2)

    @pl.when(k == 0)
    def _():
        acc_ref[...] = (r_ref[...] + part) if has_res else part

    @pl.when(k > 0)
    def _():
        acc_ref[...] += part

    if use_scratch or relu2:
        @pl.when(k == nk - 1)
        def _():
            o_ref[...] = finish(acc_ref[...])


def matmul(a, w, *, tm, tn, tk=None, residual=None, relu2=False, out_dtype=F32):
    m, kd = a.shape
    n = w.shape[1]
    tm = min(tm, m)
    tk = kd if tk is None else tk
    nk = kd // tk
    has_res = residual is not None
    use_scratch = nk > 1 and out_dtype != F32
    in_specs = [pl.BlockSpec((tm, tk), lambda i, j, k: (i, k)),
                pl.BlockSpec((tk, tn), lambda i, j, k: (k, j))]
    args = [a, w]
    if has_res:
        in_specs.append(pl.BlockSpec((tm, tn), lambda i, j, k: (i, j)))
        args.append(residual)
    return pl.pallas_call(
        functools.partial(_mm_kernel, nk=nk, relu2=relu2, has_res=has_res, use_scratch=use_scratch),
        grid=(m // tm, n // tn, nk),
        in_specs=in_specs,
        out_specs=pl.BlockSpec((tm, tn), lambda i, j, k: (i, j)),
        out_shape=jax.ShapeDtypeStruct((m, n), out_dtype),
        scratch_shapes=[pltpu.VMEM((tm, tn), F32)] if use_scratch else [],
        name="matmul",
        compiler_params=_params("parallel", "parallel", "arbitrary"),
    )(*args)


def _rope_tables(pos_list, theta, width_each):
    half = width_each // 2
    inv = theta ** (-jnp.arange(half, dtype=F32) / half)
    cos_parts, sin_parts = [], []
    for pos in pos_list:
        ang = pos[:, None] * inv[None, :]
        c, s = jnp.cos(ang), jnp.sin(ang)
        cos_parts += [c, c]
        sin_parts += [-s, s]
    length = pos_list[0].shape[0]
    rest = HEAD_DIM - width_each * len(pos_list)
    if rest:
        cos_parts.append(jnp.ones((length, rest), F32))
        sin_parts.append(jnp.zeros((length, rest), F32))
    return jnp.concatenate(cos_parts, axis=1), jnp.concatenate(sin_parts, axis=1)


def _head_prep_kernel(*refs, n_heads, half, use_norm, use_rope, scale):
    x_ref = refs[0]
    o_ref = refs[-1]
    idx = 1
    if use_norm:
        g = refs[idx][...]
        idx += 1
    if use_rope:
        cos = refs[idx][...]
        sin = refs[idx + 1][...]
        lane = lax.broadcasted_iota(jnp.int32, cos.shape, 1)
        first = (lane % (2 * half)) < half
    for h in range(n_heads):
        x = x_ref[:, h * HEAD_DIM:(h + 1) * HEAD_DIM]
        if use_norm:
            ms = jnp.mean(x * x, axis=-1, keepdims=True)
            x = x * lax.rsqrt(ms + EPS) * g
        if use_rope:
            partner = jnp.where(first, pltpu.roll(x, HEAD_DIM - half, axis=1), pltpu.roll(x, half, axis=1))
            x = x * cos + partner * sin
        if scale != 1.0:
            x = x * scale
        o_ref[h] = x.astype(o_ref.dtype)


def head_prep(proj, col_off, n_heads, *, gain=None, tables=None, half=None, scale=1.0, tm=256):
    length = proj.shape[0]
    width = n_heads * HEAD_DIM
    in_specs = [pl.BlockSpec((tm, width), lambda i: (i, col_off // width))]
    args = [proj]
    if gain is not None:
        in_specs.append(pl.BlockSpec((1, HEAD_DIM), lambda i: (0, 0)))
        args.append(gain.reshape(1, HEAD_DIM))
    if tables is not None:
        in_specs += [pl.BlockSpec((tm, HEAD_DIM), lambda i: (i, 0))] * 2
        args += list(tables)
    return pl.pallas_call(
        functools.partial(_head_prep_kernel, n_heads=n_heads, half=half, use_norm=gain is not None,
                          use_rope=tables is not None, scale=scale),
        grid=(length // tm,),
        in_specs=in_specs,
        out_specs=pl.BlockSpec((n_heads, tm, HEAD_DIM), lambda i: (0, i, 0)),
        out_shape=jax.ShapeDtypeStruct((n_heads, length, HEAD_DIM), BF16),
        name="head_prep",
        compiler_params=_params("parallel"),
    )(*args)


def _flash_kernel(q_ref, k_ref, v_ref, o_ref, m_ref, l_ref, acc_ref, *, tq, nkv, chunk, nchunk):
    ki = pl.program_id(2)

    @pl.when(ki == 0)
    def _():
        m_ref[...] = jnp.full(m_ref.shape, NEG_BIG, F32)
        l_ref[...] = jnp.zeros(l_ref.shape, F32)
        acc_ref[...] = jnp.zeros(acc_ref.shape, F32)

    q = q_ref[...].reshape(GQA_G * tq, HEAD_DIM)
    ntile = chunk // HEAD_DIM

    def scores(j):
        return lax.dot_general(q, k_ref[0, j * chunk:(j + 1) * chunk, :], (((1,), (1,)), ((), ())),
                               preferred_element_type=F32)

    m, l, acc = m_ref[...], l_ref[...], acc_ref[...]
    s_next = scores(0)
    for j in range(nchunk):
        s = s_next
        if j + 1 < nchunk:
            s_next = scores(j + 1)
        tiles = [s[:, t * HEAD_DIM:(t + 1) * HEAD_DIM] for t in range(ntile)]
        part = tiles[0]
        for t in tiles[1:]:
            part = jnp.maximum(part, t)
        m_new = jnp.maximum(m, jnp.max(part, axis=-1, keepdims=True))
        alpha = jnp.exp2(m - m_new)
        ps = [jnp.exp2(t - m_new) for t in tiles]
        psum = ps[0]
        for t in ps[1:]:
            psum = psum + t
        l = alpha * l + psum
        p = jnp.concatenate([t.astype(BF16) for t in ps], axis=1)
        acc = alpha * acc + jnp.dot(p, v_ref[0, j * chunk:(j + 1) * chunk, :], preferred_element_type=F32)
        m = m_new
    m_ref[...], l_ref[...], acc_ref[...] = m, l, acc

    @pl.when(ki == nkv - 1)
    def _():
        o = acc / jnp.sum(l, axis=-1, keepdims=True)
        for g in range(GQA_G):
            o_ref[:, g * HEAD_DIM:(g + 1) * HEAD_DIM] = o[g * tq:(g + 1) * tq].astype(o_ref.dtype)


def axial_flash_attention(q, k, v, *, tq=256, tk=4096, chunk=1024):
    length = q.shape[1]
    tk = min(tk, length)
    chunk = min(chunk, tk)
    nkv = length // tk
    rows = GQA_G * tq
    return pl.pallas_call(
        functools.partial(_flash_kernel, tq=tq, nkv=nkv, chunk=chunk, nchunk=tk // chunk),
        grid=(KV_C, length // tq, nkv),
        in_specs=[pl.BlockSpec((GQA_G, tq, HEAD_DIM), lambda h, i, j: (h, i, 0)),
                  pl.BlockSpec((1, tk, HEAD_DIM), lambda h, i, j: (h, j, 0)),
                  pl.BlockSpec((1, tk, HEAD_DIM), lambda h, i, j: (h, j, 0))],
        out_specs=pl.BlockSpec((tq, GQA_G * HEAD_DIM), lambda h, i, j: (i, h)),
        out_shape=jax.ShapeDtypeStruct((length, H_C * HEAD_DIM), BF16),
        scratch_shapes=[pltpu.VMEM((rows, HEAD_DIM), F32), pltpu.VMEM((rows, HEAD_DIM), F32),
                        pltpu.VMEM((rows, HEAD_DIM), F32)],
        name="axial_flash",
        compiler_params=_params("parallel", "parallel", "arbitrary"),
    )(q, k, v)


def _dilated_kernel(q_ref, kp_ref, kc_ref, kn_ref, vp_ref, vc_ref, vn_ref, o_ref, lse_ref, *, tq, side, n_sub):
    c = pl.program_id(1)
    row = lax.broadcasted_iota(jnp.int32, (tq, tq + 2 * side), 0)
    col = lax.broadcasted_iota(jnp.int32, (tq, tq + 2 * side), 1)
    key_idx = c * tq - side + col
    valid = (jnp.abs(row + side - col) <= side) & (key_idx >= 0) & (key_idx < n_sub)
    for h in range(H_B_GROUP):
        kk = jnp.concatenate([kp_ref[h, tq - side:, :], kc_ref[h], kn_ref[h, :side, :]], axis=0)
        vv = jnp.concatenate([vp_ref[h, tq - side:, :], vc_ref[h], vn_ref[h, :side, :]], axis=0)
        s = lax.dot_general(q_ref[h], kk, (((1,), (1,)), ((), ())), preferred_element_type=F32)
        s = jnp.where(valid, s, NEG_BIG)
        m = jnp.max(s, axis=-1, keepdims=True)
        e = jnp.where(valid, jnp.exp(s - m), 0.0)
        den = jnp.sum(e, axis=-1, keepdims=True)
        o = jnp.dot(e.astype(BF16), vv, preferred_element_type=F32) / den
        o_ref[h] = o
        lse_ref[h] = jnp.broadcast_to(m + jnp.log(den), (tq, HEAD_DIM))


def dilated_group_attention(q, k, v, group, window, dilation, *, tq=256):
    length = q.shape[1]
    d = dilation
    side = (window // 2) // d
    n_sub = length // d
    nblk = n_sub // tq
    qv, kv, vv = (t.reshape(H_B, n_sub, d * HEAD_DIM) for t in (q, k, v))
    cur = lambda r, c: (group, c, r)
    prev = lambda r, c: (group, jnp.maximum(c - 1, 0), r)
    nxt = lambda r, c: (group, jnp.minimum(c + 1, nblk - 1), r)
    blk = (H_B_GROUP, tq, HEAD_DIM)
    out, lse = pl.pallas_call(
        functools.partial(_dilated_kernel, tq=tq, side=side, n_sub=n_sub),
        grid=(d, nblk),
        in_specs=[pl.BlockSpec(blk, cur), pl.BlockSpec(blk, prev), pl.BlockSpec(blk, cur), pl.BlockSpec(blk, nxt),
                  pl.BlockSpec(blk, prev), pl.BlockSpec(blk, cur), pl.BlockSpec(blk, nxt)],
        out_specs=[pl.BlockSpec(blk, lambda r, c: (0, c, r))] * 2,
        out_shape=[jax.ShapeDtypeStruct((H_B_GROUP, n_sub, d * HEAD_DIM), F32)] * 2,
        name="dilated_attn",
        compiler_params=_params("parallel", "parallel"),
    )(qv, kv, kv, kv, vv, vv, vv)
    return out.reshape(H_B_GROUP, length, HEAD_DIM), lse.reshape(H_B_GROUP, length, HEAD_DIM)


def _conv_prep_kernel(cur_ref, prev_ref, next_ref, w_ref, o_ref, *, tm, nblk):
    i = pl.program_id(0)
    j = pl.program_id(1)
    x = cur_ref[...]
    before = jnp.where(i > 0, prev_ref[7:8, :], 0.0)
    after = jnp.where(i < nblk - 1, next_ref[0:2, :], 0.0)
    row = lax.broadcasted_iota(jnp.int32, x.shape, 0)
    xm1 = jnp.where(row == 0, before, pltpu.roll(x, 1, axis=0))
    xp1 = jnp.where(row == tm - 1, after[0:1, :], pltpu.roll(x, tm - 1, axis=0))
    xp2 = jnp.where(row == tm - 2, after[0:1, :],
                    jnp.where(row == tm - 1, after[1:2, :], pltpu.roll(x, tm - 2, axis=0)))
    w = w_ref[...]
    y = xm1 * w[0:1, :] + x * w[1:2, :] + xp1 * w[2:3, :] + xp2 * w[3:4, :]
    y = y * jax.nn.sigmoid(y)
    post = jnp.where(j == 0, SCALE, 1.0)
    for h in range(H_A):
        yh = y[:, h * HEAD_DIM:(h + 1) * HEAD_DIM]
        inv = lax.rsqrt(jnp.sum(yh * yh, axis=-1, keepdims=True) + EPS) * post
        o_ref[:, h * HEAD_DIM:(h + 1) * HEAD_DIM] = yh * jnp.where(j < 2, inv, 1.0)


def deltanet_conv_prep(proj, conv_w, *, tm=256):
    length = proj.shape[0]
    nblk = length // tm
    cb = OFF_AQ // W_A
    hb = tm // 8
    return pl.pallas_call(
        functools.partial(_conv_prep_kernel, tm=tm, nblk=nblk),
        grid=(nblk, 3),
        in_specs=[pl.BlockSpec((tm, W_A), lambda i, j: (i, cb + j)),
                  pl.BlockSpec((8, W_A), lambda i, j: (jnp.maximum(i * hb - 1, 0), cb + j)),
                  pl.BlockSpec((8, W_A), lambda i, j: (jnp.minimum((i + 1) * hb, nblk * hb - 1), cb + j)),
                  pl.BlockSpec((CONV_W, W_A), lambda i, j: (0, j))],
        out_specs=pl.BlockSpec((tm, W_A), lambda i, j: (i, j)),
        out_shape=jax.ShapeDtypeStruct((length, 3 * W_A), F32),
        name="delta_conv_prep",
        compiler_params=_params("parallel", "parallel"),
    )(proj, proj, proj, conv_w)


def _gate_prep_kernel(x_ref, a_ref, dt_ref, o_ref):
    x = x_ref[...]
    lane = lax.broadcasted_iota(jnp.int32, x.shape, 1)
    beta = jax.nn.sigmoid(x)
    z = x + dt_ref[...]
    softplus = jnp.maximum(z, 0.0) + jnp.log(1.0 + jnp.exp(-jnp.abs(z)))
    g = -jnp.exp(a_ref[...]) * softplus
    o_ref[...] = jnp.where(lane < 2 * H_A, beta, jnp.where(lane < 4 * H_A, g, 0.0))


def deltanet_gate_prep(proj, a_log, dt_bias, *, tm=512):
    length = proj.shape[0]
    pad = lambda t: jnp.zeros((1, HEAD_DIM), F32).at[0, 2 * H_A:4 * H_A].set(t.reshape(-1))
    return pl.pallas_call(
        _gate_prep_kernel,
        grid=(length // tm,),
        in_specs=[pl.BlockSpec((tm, HEAD_DIM), lambda i: (i, OFF_SMALL // HEAD_DIM)),
                  pl.BlockSpec((1, HEAD_DIM), lambda i: (0, 0)),
                  pl.BlockSpec((1, HEAD_DIM), lambda i: (0, 0))],
        out_specs=pl.BlockSpec((tm, HEAD_DIM), lambda i: (i, 0)),
        out_shape=jax.ShapeDtypeStruct((length, HEAD_DIM), F32),
        name="delta_gate_prep",
        compiler_params=_params("parallel"),
    )(proj, pad(a_log), pad(dt_bias))


def _bdot(a, b):
    return jnp.dot(a.astype(BF16), b.astype(BF16), preferred_element_type=F32)


def _split_dot(tri, x):
    hi = x.astype(BF16)
    lo = (x - hi.astype(F32)).astype(BF16)
    return jnp.dot(tri, hi, preferred_element_type=F32) + jnp.dot(tri, lo, preferred_element_type=F32)


def _delta_kernel(qf, kf, vf, gf, qb, kb, vb, gbk, of_ref, ob_ref, s_ref, *, chunk):
    @pl.when(pl.program_id(0) == 0)
    def _():
        s_ref[...] = jnp.zeros(s_ref.shape, F32)

    cc = chunk
    row = lax.broadcasted_iota(jnp.int32, (cc, cc), 0)
    col = lax.broadcasted_iota(jnp.int32, (cc, cc), 1)
    eye = (row == col).astype(F32)
    level_masks = []
    differ = row ^ col
    b = 2
    while b <= cc:
        level_masks.append((differ < b) & (differ >= b // 2))
        b *= 2

    chains = []
    for direction, (q_ref, k_ref, v_ref, gb_ref, o_ref) in enumerate(
            ((qf, kf, vf, gf, of_ref), (qb, kb, vb, gbk, ob_ref))):
        incl = (row >= col) if direction == 0 else (row <= col)
        strict = (row > col) if direction == 0 else (row < col)
        last = cc - 1 if direction == 0 else 0
        gb = gb_ref[...]
        gc_all = _split_dot(incl.astype(BF16), gb)
        gc_rows = jnp.concatenate([gc_all, jnp.zeros_like(gc_all)], axis=0).T if cc < HEAD_DIM else gc_all.T
        for h in range(H_A):
            lane_beta = direction * H_A + h
            lane_g = 2 * H_A + direction * H_A + h
            sl = slice(h * HEAD_DIM, (h + 1) * HEAD_DIM)
            chains.append(dict(
                q=q_ref[:, sl], k=k_ref[:, sl], v=v_ref[:, sl], o_ref=o_ref, sl=sl, idx=direction * H_A + h,
                beta=gb[:, lane_beta:lane_beta + 1],
                gcc=gc_all[:, lane_g:lane_g + 1],
                gcr=gc_rows[lane_g:lane_g + 1, 0:cc],
                incl=incl, strict=strict, last=last))

    for c in chains:
        c["decay"] = jnp.exp(jnp.where(c["incl"], c["gcc"] - c["gcr"], NEG_BIG))
        c["egc"] = jnp.exp(c["gcc"])
        c["kb"] = c["k"] * c["beta"]
        c["vb"] = c["v"] * c["beta"]
    nt = (((1,), (1,)), ((), ()))
    kks = [lax.dot_general(jnp.concatenate([c["kb"], c["q"]], axis=0).astype(BF16), c["k"].astype(BF16), nt,
                           preferred_element_type=F32) for c in chains]
    for c, kk in zip(chains, kks):
        c["a"] = jnp.where(c["strict"], kk[:cc] * c["decay"], 0.0)
        c["qk"] = kk[cc:] * c["decay"]
    ts = [eye - jnp.where(level_masks[0], c["a"], 0.0) for c in chains]
    for mask in level_masks[1:]:
        tb = [_bdot(t, jnp.where(mask, c["a"], 0.0)) for t, c in zip(ts, chains)]
        tbt = [_bdot(x, t) for x, t in zip(tb, ts)]
        ts = [t - x for t, x in zip(ts, tbt)]
    uws = [_bdot(t, jnp.concatenate([c["vb"], c["kb"] * c["egc"]], axis=1)) for t, c in zip(ts, chains)]
    s_olds = [s_ref[c["idx"]] for c in chains]
    wqs = [_bdot(jnp.concatenate([uw[:, HEAD_DIM:], c["q"] * c["egc"]], axis=0), s_old)
           for uw, c, s_old in zip(uws, chains, s_olds)]
    v_news = [uw[:, :HEAD_DIM] - wq[:cc] for uw, wq in zip(uws, wqs)]
    intra = [_bdot(c["qk"], v_new) for c, v_new in zip(chains, v_news)]
    for c, wq, x in zip(chains, wqs, intra):
        c["o_ref"][:, c["sl"]] = wq[cc:] + x
    tn = (((0,), (0,)), ((), ()))
    g_lasts = [c["gcc"][c["last"]:c["last"] + 1, :] for c in chains]
    kvs = [lax.dot_general((c["k"] * jnp.exp(g_last - c["gcc"])).astype(BF16), v_new.astype(BF16), tn,
                           preferred_element_type=F32) for c, g_last, v_new in zip(chains, g_lasts, v_news)]
    for c, s_old, g_last, kv in zip(chains, s_olds, g_lasts, kvs):
        s_ref[c["idx"]] = s_old * jnp.exp(g_last) + kv


def deltanet_recurrence(qkv, gb, *, chunk=DN_CHUNK):
    length = qkv.shape[0]
    n = length // chunk
    fwd = lambda j: (lambda c: (c, j))
    bwd = lambda j: (lambda c: (n - 1 - c, j))
    col = lambda m, width: pl.BlockSpec((chunk, width), m)
    return pl.pallas_call(
        functools.partial(_delta_kernel, chunk=chunk),
        grid=(n,),
        in_specs=[col(fwd(0), W_A), col(fwd(1), W_A), col(fwd(2), W_A), col(fwd(0), HEAD_DIM),
                  col(bwd(0), W_A), col(bwd(1), W_A), col(bwd(2), W_A), col(bwd(0), HEAD_DIM)],
        out_specs=[col(fwd(0), W_A), col(bwd(0), W_A)],
        out_shape=[jax.ShapeDtypeStruct((length, W_A), F32)] * 2,
        scratch_shapes=[pltpu.VMEM((2 * H_A, HEAD_DIM, HEAD_DIM), F32)],
        name="delta_recurrence",
        compiler_params=_params("arbitrary"),
    )(qkv, qkv, qkv, gb, qkv, qkv, qkv, gb)


def _assemble_kernel(of_ref, ob_ref, gate_ref, og_ref, o0, o1, o2, l0, l1, l2, yc_ref, mix_ref):
    og = og_ref[...]
    for h in range(H_A):
        sl = slice(h * HEAD_DIM, (h + 1) * HEAD_DIM)
        o = of_ref[:, sl] + ob_ref[:, sl]
        ms = jnp.mean(o * o, axis=-1, keepdims=True)
        gate = gate_ref[:, sl]
        mix_ref[:, sl] = (o * lax.rsqrt(ms + EPS) * og * (gate * jax.nn.sigmoid(gate))).astype(mix_ref.dtype)
    outs, lses = (o0, o1, o2), (l0, l1, l2)
    for h in range(H_B_GROUP):
        ls = [r[h] for r in lses]
        mx = jnp.maximum(jnp.maximum(ls[0], ls[1]), ls[2])
        es = [jnp.exp(t - mx) for t in ls]
        tot = es[0] + es[1] + es[2]
        for gi in range(len(DILATED_PAIRS)):
            c0 = W_A + (gi * H_B_GROUP + h) * HEAD_DIM
            mix_ref[:, c0:c0 + HEAD_DIM] = (outs[gi][h] * (es[gi] / tot)).astype(mix_ref.dtype)
    mix_ref[:, W_A + W_B:] = yc_ref[...]


def assemble_mix(o_fwd, o_bwd, proj, onorm_g, dil_outs, dil_lses, y_c, *, tm=256):
    length = o_fwd.shape[0]
    row = lambda w: pl.BlockSpec((tm, w), lambda i: (i, 0))
    hblk = pl.BlockSpec((H_B_GROUP, tm, HEAD_DIM), lambda i: (0, i, 0))
    return pl.pallas_call(
        _assemble_kernel,
        grid=(length // tm,),
        in_specs=[row(W_A), row(W_A), pl.BlockSpec((tm, W_A), lambda i: (i, OFF_AGATE // W_A)),
                  pl.BlockSpec((1, HEAD_DIM), lambda i: (0, 0))] + [hblk] * 6 + [row(W_CQ)],
        out_specs=row(D_MODEL),
        out_shape=jax.ShapeDtypeStruct((length, D_MODEL), BF16),
        name="assemble_mix",
        compiler_params=_params("parallel"),
    )(o_fwd, o_bwd, proj, onorm_g.reshape(1, HEAD_DIM), *dil_outs, *dil_lses, y_c)


def _cross_kernel(x_ref, gx_ref, wq_ref, k_ref, v_ref, wo_ref, gf_ref, xo_ref, hf_ref):
    x = x_ref[...]
    ms = jnp.mean(x * x, axis=-1, keepdims=True)
    h = (x * lax.rsqrt(ms + EPS) * gx_ref[...]).astype(BF16)
    q = jnp.dot(h, wq_ref[...], preferred_element_type=F32)
    heads = []
    for hh in range(X_HEADS):
        sl = slice(hh * HEAD_DIM, (hh + 1) * HEAD_DIM)
        s = lax.dot_general(q[:, sl].astype(BF16), k_ref[:, sl], (((1,), (1,)), ((), ())),
                            preferred_element_type=F32) * SCALE
        m = jnp.max(s, axis=-1, keepdims=True)
        e = jnp.exp(s - m)
        p = e / jnp.sum(e, axis=-1, keepdims=True)
        heads.append(jnp.dot(p.astype(BF16), v_ref[:, sl], preferred_element_type=F32).astype(BF16))
    o = jnp.concatenate(heads, axis=1)
    xn = x + jnp.dot(o, wo_ref[...], preferred_element_type=F32)
    xo_ref[...] = xn
    ms2 = jnp.mean(xn * xn, axis=-1, keepdims=True)
    hf_ref[...] = (xn * lax.rsqrt(ms2 + EPS) * gf_ref[...]).astype(hf_ref.dtype)


def cross_attention_block(x, g_x, wq, kv, wo, g_ffn, *, tm=256):
    length = x.shape[0]
    width = X_HEADS * HEAD_DIM
    full = lambda shape: pl.BlockSpec(shape, lambda i: (0, 0))
    row = pl.BlockSpec((tm, D_MODEL), lambda i: (i, 0))
    return pl.pallas_call(
        _cross_kernel,
        grid=(length // tm,),
        in_specs=[row, full((1, D_MODEL)), full((D_MODEL, width)),
                  pl.BlockSpec((N_MEM, width), lambda i: (0, 0)), pl.BlockSpec((N_MEM, width), lambda i: (0, 1)),
                  full((width, D_MODEL)), full((1, D_MODEL))],
        out_specs=[row, row],
        out_shape=[jax.ShapeDtypeStruct((length, D_MODEL), F32), jax.ShapeDtypeStruct((length, D_MODEL), BF16)],
        name="cross_attn",
        compiler_params=_params("parallel"),
    )(x, g_x.reshape(1, D_MODEL), wq, kv, kv, wo, g_ffn.reshape(1, D_MODEL))


def _permute_w_in(w):
    sizes = (3 * W_A, W_A, 2 * H_A, 2 * H_A, 3 * W_B, W_CQ, W_CKV, W_CKV)
    offs = [0]
    for s in sizes:
        offs.append(offs[-1] + s)
    a_qkv, a_gate, a_beta, a_decay, b_qkv, c_q, c_k, c_v = (w[:, offs[i]:offs[i + 1]] for i in range(8))
    pad = jnp.zeros((w.shape[0], W_SMALL - 4 * H_A), w.dtype)
    return jnp.concatenate([c_q, b_qkv, c_k, c_v, a_qkv, a_gate, a_beta, a_decay, pad], axis=1).astype(BF16)


def _encode(x, mem, tables_b, tables_c, p):
    for l in range(len(p["w_in"])):
        h = rmsnorm_rows(x, p["norm_mix_g"][l], BF16)
        proj = matmul(h, p["w_in"][l], tm=1024, tn=512)
        qkv_a = deltanet_conv_prep(proj, p["conv_w"][l])
        gb = deltanet_gate_prep(proj, p["a_log"][l], p["dt_bias"][l])
        o_fwd, o_bwd = deltanet_recurrence(qkv_a, gb)
        q_b = head_prep(proj, OFF_BQ, H_B, tables=tables_b, half=ROT_DIMS // 2, scale=SCALE)
        k_b = head_prep(proj, OFF_BK, H_B, tables=tables_b, half=ROT_DIMS // 2)
        v_b = head_prep(proj, OFF_BV, H_B)
        dil = [dilated_group_attention(q_b, k_b, v_b, gi, w, d) for gi, (w, d) in enumerate(DILATED_PAIRS)]
        q_c = head_prep(proj, OFF_CQ, H_C, gain=p["qnorm_g"][l], tables=tables_c, half=HEAD_DIM // 4,
                        scale=SCALE * LOG2_E)
        k_c = head_prep(proj, OFF_CK, KV_C, gain=p["knorm_g"][l], tables=tables_c, half=HEAD_DIM // 4)
        v_c = head_prep(proj, OFF_CV, KV_C)
        y_c = axial_flash_attention(q_c, k_c, v_c)
        mix = assemble_mix(o_fwd, o_bwd, proj, p["onorm_g"][l], [t[0] for t in dil], [t[1] for t in dil], y_c)
        x = matmul(mix, p["w_out"][l], tm=1024, tn=512, residual=x)
        mem_n = rmsnorm_rows(mem, p["norm_mem_g"][l], BF16)
        kv = matmul(mem_n, p["wkv_x"][l], tm=N_MEM, tn=512, out_dtype=BF16)
        x, hf = cross_attention_block(x, p["norm_x_g"][l], p["wq_x"][l], kv, p["wo_x"][l], p["norm_ffn_g"][l])
        up = matmul(hf, p["w_up"][l], tm=1024, tn=1024, relu2=True, out_dtype=BF16)
        x = matmul(up, p["w_down"][l], tm=1024, tn=1024, tk=2048, residual=x)
    return rmsnorm_rows(x, p["norm_final_g"], F32)


def _tables(length):
    pos = jnp.arange(length, dtype=F32)
    tables_b = _rope_tables([pos], ROPE_THETA, ROT_DIMS)
    rows = length // GRID_W
    row_pos = jnp.repeat(jnp.arange(rows, dtype=F32), GRID_W)
    col_pos = jnp.tile(jnp.arange(GRID_W, dtype=F32), rows)
    tables_c = _rope_tables([row_pos, col_pos], AXIAL_THETA, HEAD_DIM // 2)
    return tables_b, tables_c


def kernel(x_prompt, x_sample, mem_prompt, mem_sample, norm_mix_g, w_in, conv_w, a_log, dt_bias, onorm_g, qnorm_g, knorm_g, w_out, norm_x_g, norm_mem_g, wq_x, wk_x, wv_x, wo_x, norm_ffn_g, w_up, w_down, norm_final_g):
    depth = w_in.shape[0]
    p = dict(
        norm_mix_g=norm_mix_g, conv_w=conv_w, a_log=a_log, dt_bias=dt_bias, onorm_g=onorm_g, qnorm_g=qnorm_g,
        knorm_g=knorm_g, norm_x_g=norm_x_g, norm_mem_g=norm_mem_g, norm_ffn_g=norm_ffn_g, norm_final_g=norm_final_g,
        w_in=[_permute_w_in(w_in[l]) for l in range(depth)],
        w_out=w_out.astype(BF16), wq_x=wq_x.astype(BF16),
        wkv_x=jnp.concatenate([wk_x, wv_x], axis=-1).astype(BF16),
        wo_x=wo_x.astype(BF16), w_up=w_up.astype(BF16), w_down=w_down.astype(BF16),
    )
    outs = []
    for x, mem in ((x_prompt, mem_prompt), (x_sample, mem_sample)):
        length = x.shape[1]
        tables_b, tables_c = _tables(length)
        y = _encode(x[0], mem[0], tables_b, tables_c, p)
        outs.append(y[None])
    return tuple(outs)
```

```python
import functools
import math

import jax
import jax.numpy as jnp
from jax import lax
from jax.experimental import pallas as pl
from jax.experimental.pallas import tpu as pltpu

F32 = jnp.float32
BF16 = jnp.bfloat16

D_MODEL = 4096
HEAD_DIM = 128
H_A = 8
DILATED_PAIRS = ((128, 1), (512, 4), (2048, 16))
H_B_GROUP = 4
H_B = 12
H_C = 12
KV_C = 4
GQA_G = H_C // KV_C
CONV_W = 4
ROPE_THETA = 500000.0
ROT_DIMS = HEAD_DIM // 4
AXIAL_THETA = 10000.0
GRID_W = 64
N_MEM = 256
X_HEADS = 4
D_FF = 4 * D_MODEL
EPS = 1e-6
SCALE = HEAD_DIM ** -0.5
LOG2_E = math.log2(math.e)
NEG_BIG = -1e30

W_A = H_A * HEAD_DIM
W_B = H_B * HEAD_DIM
W_CQ = H_C * HEAD_DIM
W_CKV = KV_C * HEAD_DIM
OFF_CQ = 0
OFF_BQ = OFF_CQ + W_CQ
OFF_BK = OFF_BQ + W_B
OFF_BV = OFF_BK + W_B
OFF_CK = OFF_BV + W_B
OFF_CV = OFF_CK + W_CKV
OFF_AQ = OFF_CV + W_CKV
OFF_AGATE = OFF_AQ + 3 * W_A
OFF_SMALL = OFF_AGATE + W_A
W_SMALL = 512
PROJ_PAD = OFF_SMALL + W_SMALL

V7X_VMEM_LIMIT_BYTES = 56 * 1024 * 1024
DN_CHUNK = 64


def _params(*sem):
    return pltpu.CompilerParams(dimension_semantics=sem, vmem_limit_bytes=V7X_VMEM_LIMIT_BYTES)


def _rmsnorm_kernel(x_ref, g_ref, o_ref):
    x = x_ref[...]
    ms = jnp.mean(x * x, axis=-1, keepdims=True)
    o_ref[...] = (x * lax.rsqrt(ms + EPS) * g_ref[...]).astype(o_ref.dtype)


def rmsnorm_rows(x, g, out_dtype, tm=256):
    m, d = x.shape
    return pl.pallas_call(
        _rmsnorm_kernel,
        grid=(m // tm,),
        in_specs=[pl.BlockSpec((tm, d), lambda i: (i, 0)), pl.BlockSpec((1, d), lambda i: (0, 0))],
        out_specs=pl.BlockSpec((tm, d), lambda i: (i, 0)),
        out_shape=jax.ShapeDtypeStruct((m, d), out_dtype),
        name="rmsnorm_rows",
        compiler_params=_params("parallel"),
    )(x, g.reshape(1, d))


def _mm_kernel(*refs, nk, relu2, has_res, use_scratch):
    a_ref, w_ref = refs[0], refs[1]
    r_ref = refs[2] if has_res else None
    o_ref = refs[3] if has_res else refs[2]
    def part():
        return jnp.dot(a_ref[...], w_ref[...], preferred_element_type=F32)

    def finish(acc):
        if relu2:
            acc = jnp.square(jnp.maximum(acc, 0.0))
        return acc.astype(o_ref.dtype)

    if nk == 1:
        o_ref[...] = finish(r_ref[...] + part() if has_res else part())
        return
    acc_ref = refs[-1] if use_scratch else o_ref
    k = pl.program_id(2)

    @pl.when(k == 0)
    def _():
        acc_ref[...] = (r_ref[...] + part()) if has_res else part()

    @pl.when(k > 0)
    def _():
        acc_ref[...] = acc_ref[...] + part()

    if use_scratch or relu2:
        @pl.when(k == nk - 1)
        def _():
            o_ref[...] = finish(acc_ref[...])


def matmul(a, w, layer, *, tm, tn, tk=None, residual=None, relu2=False, out_dtype=F32):
    m, kd = a.shape
    n = w.shape[2]
    tm = min(tm, m)
    tk = kd if tk is None else tk
    nk = kd // tk
    has_res = residual is not None
    use_scratch = nk > 1 and out_dtype != F32
    in_specs = [pl.BlockSpec((tm, tk), lambda i, j, k: (i, k)),
                pl.BlockSpec((None, tk, tn), lambda i, j, k: (layer, k, j))]
    args = [a, w]
    if has_res:
        in_specs.append(pl.BlockSpec((tm, tn), lambda i, j, k: (i, j)))
        args.append(residual)
    return pl.pallas_call(
        functools.partial(_mm_kernel, nk=nk, relu2=relu2, has_res=has_res, use_scratch=use_scratch),
        grid=(m // tm, n // tn, nk),
        in_specs=in_specs,
        out_specs=pl.BlockSpec((tm, tn), lambda i, j, k: (i, j)),
        out_shape=jax.ShapeDtypeStruct((m, n), out_dtype),
        scratch_shapes=[pltpu.VMEM((tm, tn), F32)] if use_scratch else [],
        name="matmul",
        compiler_params=_params("parallel", "parallel", "arbitrary"),
    )(*args)


def _rope_tables(pos_list, theta, width_each):
    half = width_each // 2
    inv = theta ** (-jnp.arange(half, dtype=F32) / half)
    cos_parts, sin_parts = [], []
    for pos in pos_list:
        ang = pos[:, None] * inv[None, :]
        c, s = jnp.cos(ang), jnp.sin(ang)
        cos_parts += [c, c]
        sin_parts += [-s, s]
    length = pos_list[0].shape[0]
    rest = HEAD_DIM - width_each * len(pos_list)
    if rest:
        cos_parts.append(jnp.ones((length, rest), F32))
        sin_parts.append(jnp.zeros((length, rest), F32))
    return jnp.concatenate(cos_parts, axis=1), jnp.concatenate(sin_parts, axis=1)


def _head_prep_kernel(*refs, groups, tm, half, use_norm, use_rope, scale):
    x_ref = refs[0]
    n_out = len(groups)
    o_refs = refs[-n_out - 1:-1]
    buf_ref = refs[-1]
    idx = 1
    if use_norm:
        g = refs[idx][...]
        idx += 1
    if use_rope:
        cos = refs[idx][...]
        sin = refs[idx + 1][...]
        lane = lax.broadcasted_iota(jnp.int32, cos.shape, 1)
        first = (lane % (2 * half)) < half
    h = 0
    for o_ref, (n_heads, dilation) in zip(o_refs, groups):
        for hg in range(n_heads):
            x = x_ref[:, h * HEAD_DIM:(h + 1) * HEAD_DIM]
            if use_norm:
                ms = jnp.mean(x * x, axis=-1, keepdims=True)
                x = x * lax.rsqrt(ms + EPS) * g
            if use_rope:
                partner = jnp.where(first, pltpu.roll(x, HEAD_DIM - half, axis=1), pltpu.roll(x, half, axis=1))
                x = x * cos + partner * sin
            if scale != 1.0:
                x = x * scale
            if dilation == 1:
                o_ref[hg] = x.astype(o_ref.dtype)
            else:
                buf_ref[h] = x
                for r in range(dilation):
                    o_ref[hg, :, r * HEAD_DIM:(r + 1) * HEAD_DIM] = buf_ref[
                        h, pl.ds(r, tm // dilation, stride=dilation), :].astype(o_ref.dtype)
            h += 1


def head_prep(proj, col_off, groups, *, gain=None, tables=None, half=None, scale=1.0, tm=512):
    length = proj.shape[0]
    n_heads = sum(n for n, _ in groups)
    width = n_heads * HEAD_DIM
    in_specs = [pl.BlockSpec((tm, width), lambda i: (i, col_off // width))]
    args = [proj]
    if gain is not None:
        in_specs.append(pl.BlockSpec((1, HEAD_DIM), lambda i: (0, 0)))
        args.append(gain.reshape(1, HEAD_DIM))
    if tables is not None:
        in_specs += [pl.BlockSpec((tm, HEAD_DIM), lambda i: (i, 0))] * 2
        args += list(tables)
    outs = pl.pallas_call(
        functools.partial(_head_prep_kernel, groups=groups, tm=tm, half=half, use_norm=gain is not None,
                          use_rope=tables is not None, scale=scale),
        grid=(length // tm,),
        in_specs=in_specs,
        out_specs=[pl.BlockSpec((n, tm // d, d * HEAD_DIM), lambda i: (0, i, 0)) for n, d in groups],
        out_shape=[jax.ShapeDtypeStruct((n, length // d, d * HEAD_DIM), BF16) for n, d in groups],
        scratch_shapes=[pltpu.VMEM((n_heads, tm, HEAD_DIM), F32)],
        name="head_prep",
        compiler_params=_params("parallel"),
    )(*args)
    return outs


def _flash_kernel(q_ref, k_ref, v_ref, o_ref, m_ref, l_ref, acc_ref, *, tq, nkv, chunk, nchunk):
    ki = pl.program_id(2)

    @pl.when(ki == 0)
    def _():
        m_ref[...] = jnp.full(m_ref.shape, NEG_BIG, F32)
        l_ref[...] = jnp.zeros(l_ref.shape, F32)
        acc_ref[...] = jnp.zeros(acc_ref.shape, F32)

    q = q_ref[...].reshape(GQA_G * tq, HEAD_DIM)
    ntile = chunk // HEAD_DIM

    def scores(j):
        return lax.dot_general(q, k_ref[0, j * chunk:(j + 1) * chunk, :], (((1,), (1,)), ((), ())),
                               preferred_element_type=F32)

    m, l, acc = m_ref[...], l_ref[...], acc_ref[...]
    s_next = scores(0)
    for j in range(nchunk):
        s = s_next
        if j + 1 < nchunk:
            s_next = scores(j + 1)
        tiles = [s[:, t * HEAD_DIM:(t + 1) * HEAD_DIM] for t in range(ntile)]
        part = tiles[0]
        for t in tiles[1:]:
            part = jnp.maximum(part, t)
        m_new = jnp.maximum(m, jnp.max(part, axis=-1, keepdims=True))
        alpha = jnp.exp2(m - m_new)
        ps = [jnp.exp2(t - m_new) for t in tiles]
        psum = ps[0]
        for t in ps[1:]:
            psum = psum + t
        l = alpha * l + psum
        p = jnp.concatenate([t.astype(BF16) for t in ps], axis=1)
        acc = alpha * acc + jnp.dot(p, v_ref[0, j * chunk:(j + 1) * chunk, :], preferred_element_type=F32)
        m = m_new
    m_ref[...], l_ref[...], acc_ref[...] = m, l, acc

    @pl.when(ki == nkv - 1)
    def _():
        o = acc / jnp.sum(l, axis=-1, keepdims=True)
        for g in range(GQA_G):
            o_ref[:, g * HEAD_DIM:(g + 1) * HEAD_DIM] = o[g * tq:(g + 1) * tq].astype(o_ref.dtype)


def axial_flash_attention(q, k, v, *, tq=256, tk=8192, chunk=512):
    length = q.shape[1]
    tk = min(tk, length)
    chunk = min(chunk, tk)
    nkv = length // tk
    rows = GQA_G * tq
    return pl.pallas_call(
        functools.partial(_flash_kernel, tq=tq, nkv=nkv, chunk=chunk, nchunk=tk // chunk),
        grid=(KV_C, length // tq, nkv),
        in_specs=[pl.BlockSpec((GQA_G, tq, HEAD_DIM), lambda h, i, j: (h, i, 0)),
                  pl.BlockSpec((1, tk, HEAD_DIM), lambda h, i, j: (h, j, 0)),
                  pl.BlockSpec((1, tk, HEAD_DIM), lambda h, i, j: (h, j, 0))],
        out_specs=pl.BlockSpec((tq, GQA_G * HEAD_DIM), lambda h, i, j: (i, h)),
        out_shape=jax.ShapeDtypeStruct((length, H_C * HEAD_DIM), BF16),
        scratch_shapes=[pltpu.VMEM((rows, HEAD_DIM), F32), pltpu.VMEM((rows, HEAD_DIM), F32),
                        pltpu.VMEM((rows, HEAD_DIM), F32)],
        name="axial_flash",
        compiler_params=_params("parallel", "parallel", "arbitrary"),
    )(q, k, v)


def _dilated_kernel(q_ref, kp_ref, kc_ref, kn_ref, vp_ref, vc_ref, vn_ref, o_ref, lse_ref, *, tq, side, n_sub):
    c = pl.program_id(1)
    row = lax.broadcasted_iota(jnp.int32, (tq, tq + 2 * side), 0)
    col = lax.broadcasted_iota(jnp.int32, (tq, tq + 2 * side), 1)
    key_idx = c * tq - side + col
    valid = (jnp.abs(row + side - col) <= side) & (key_idx >= 0) & (key_idx < n_sub)
    heads = range(H_B_GROUP)
    halo = lambda p_ref, c_ref, n_ref, h: jnp.concatenate([p_ref[h, tq - side:, :], c_ref[h], n_ref[h, :side, :]], axis=0)
    ss = [lax.dot_general(q_ref[h], halo(kp_ref, kc_ref, kn_ref, h), (((1,), (1,)), ((), ())),
                          preferred_element_type=F32) for h in heads]
    ss = [jnp.where(valid, s, NEG_BIG) for s in ss]
    ms = [jnp.max(s, axis=-1, keepdims=True) for s in ss]
    es = [jnp.exp(s - m) for s, m in zip(ss, ms)]
    dens = [jnp.sum(e, axis=-1, keepdims=True) for e in es]
    os_ = [jnp.dot(e.astype(BF16), halo(vp_ref, vc_ref, vn_ref, h), preferred_element_type=F32)
           for e, h in zip(es, heads)]
    for h in heads:
        o_ref[h] = os_[h] / dens[h]
        lse_ref[h] = jnp.broadcast_to(ms[h] + jnp.log(dens[h]), (tq, HEAD_DIM))


def dilated_group_attention(q, k, v, window, dilation, *, tq=256):
    d = dilation
    n_sub = q.shape[1]
    side = (window // 2) // d
    nblk = n_sub // tq
    cur = lambda r, c: (0, c, r)
    prev = lambda r, c: (0, jnp.maximum(c - 1, 0), r)
    nxt = lambda r, c: (0, jnp.minimum(c + 1, nblk - 1), r)
    blk = (H_B_GROUP, tq, HEAD_DIM)
    return pl.pallas_call(
        functools.partial(_dilated_kernel, tq=tq, side=side, n_sub=n_sub),
        grid=(d, nblk),
        in_specs=[pl.BlockSpec(blk, cur), pl.BlockSpec(blk, prev), pl.BlockSpec(blk, cur), pl.BlockSpec(blk, nxt),
                  pl.BlockSpec(blk, prev), pl.BlockSpec(blk, cur), pl.BlockSpec(blk, nxt)],
        out_specs=[pl.BlockSpec(blk, cur)] * 2,
        out_shape=[jax.ShapeDtypeStruct((H_B_GROUP, n_sub, d * HEAD_DIM), F32)] * 2,
        name="dilated_attn",
        compiler_params=_params("parallel", "parallel"),
    )(q, k, k, k, v, v, v)


def _conv_prep_kernel(cur_ref, prev_ref, next_ref, w_ref, o_ref, *, tm, nblk):
    i = pl.program_id(0)
    j = pl.program_id(1)
    x = cur_ref[...]
    before = jnp.where(i > 0, prev_ref[7:8, :], 0.0)
    after = jnp.where(i < nblk - 1, next_ref[0:2, :], 0.0)
    row = lax.broadcasted_iota(jnp.int32, x.shape, 0)
    xm1 = jnp.where(row == 0, before, pltpu.roll(x, 1, axis=0))
    xp1 = jnp.where(row == tm - 1, after[0:1, :], pltpu.roll(x, tm - 1, axis=0))
    xp2 = jnp.where(row == tm - 2, after[0:1, :],
                    jnp.where(row == tm - 1, after[1:2, :], pltpu.roll(x, tm - 2, axis=0)))
    w = w_ref[...]
    y = xm1 * w[0:1, :] + x * w[1:2, :] + xp1 * w[2:3, :] + xp2 * w[3:4, :]
    y = y * jax.nn.sigmoid(y)
    post = jnp.where(j == 0, SCALE, 1.0)
    for h in range(H_A):
        yh = y[:, h * HEAD_DIM:(h + 1) * HEAD_DIM]
        inv = lax.rsqrt(jnp.sum(yh * yh, axis=-1, keepdims=True) + EPS) * post
        o_ref[:, h * HEAD_DIM:(h + 1) * HEAD_DIM] = yh * jnp.where(j < 2, inv, 1.0)


def deltanet_conv_prep(proj, conv_w, *, tm=256):
    length = proj.shape[0]
    nblk = length // tm
    cb = OFF_AQ // W_A
    hb = tm // 8
    return pl.pallas_call(
        functools.partial(_conv_prep_kernel, tm=tm, nblk=nblk),
        grid=(nblk, 3),
        in_specs=[pl.BlockSpec((tm, W_A), lambda i, j: (i, cb + j)),
                  pl.BlockSpec((8, W_A), lambda i, j: (jnp.maximum(i * hb - 1, 0), cb + j)),
                  pl.BlockSpec((8, W_A), lambda i, j: (jnp.minimum((i + 1) * hb, nblk * hb - 1), cb + j)),
                  pl.BlockSpec((CONV_W, W_A), lambda i, j: (0, j))],
        out_specs=pl.BlockSpec((tm, W_A), lambda i, j: (i, j)),
        out_shape=jax.ShapeDtypeStruct((length, 3 * W_A), F32),
        name="delta_conv_prep",
        compiler_params=_params("parallel", "parallel"),
    )(proj, proj, proj, conv_w)


def _gate_prep_kernel(x_ref, a_ref, dt_ref, o_ref):
    x = x_ref[...]
    lane = lax.broadcasted_iota(jnp.int32, x.shape, 1)
    beta = jax.nn.sigmoid(x)
    z = x + dt_ref[...]
    softplus = jnp.maximum(z, 0.0) + jnp.log(1.0 + jnp.exp(-jnp.abs(z)))
    g = -jnp.exp(a_ref[...]) * softplus
    o_ref[...] = jnp.where(lane < 2 * H_A, beta, jnp.where(lane < 4 * H_A, g, 0.0))


def deltanet_gate_prep(proj, a_log, dt_bias, *, tm=512):
    length = proj.shape[0]
    pad = lambda t: jnp.zeros((1, HEAD_DIM), F32).at[0, 2 * H_A:4 * H_A].set(t.reshape(-1))
    return pl.pallas_call(
        _gate_prep_kernel,
        grid=(length // tm,),
        in_specs=[pl.BlockSpec((tm, HEAD_DIM), lambda i: (i, OFF_SMALL // HEAD_DIM)),
                  pl.BlockSpec((1, HEAD_DIM), lambda i: (0, 0)),
                  pl.BlockSpec((1, HEAD_DIM), lambda i: (0, 0))],
        out_specs=pl.BlockSpec((tm, HEAD_DIM), lambda i: (i, 0)),
        out_shape=jax.ShapeDtypeStruct((length, HEAD_DIM), F32),
        name="delta_gate_prep",
        compiler_params=_params("parallel"),
    )(proj, pad(a_log), pad(dt_bias))


def _bdot(a, b):
    return jnp.dot(a.astype(BF16), b.astype(BF16), preferred_element_type=F32)


def _split_dot(tri, x):
    hi = x.astype(BF16)
    lo = (x - hi.astype(F32)).astype(BF16)
    return jnp.dot(tri, hi, preferred_element_type=F32) + jnp.dot(tri, lo, preferred_element_type=F32)


def _delta_kernel(qf, kf, vf, gf, qb, kb, vb, gbk, of_ref, ob_ref, s_ref, *, chunk):
    @pl.when(pl.program_id(0) == 0)
    def _():
        s_ref[...] = jnp.zeros(s_ref.shape, F32)

    cc = chunk
    row = lax.broadcasted_iota(jnp.int32, (cc, cc), 0)
    col = lax.broadcasted_iota(jnp.int32, (cc, cc), 1)
    eye = (row == col).astype(F32)
    level_masks = []
    differ = row ^ col
    b = 2
    while b <= cc:
        level_masks.append((differ < b) & (differ >= b // 2))
        b *= 2

    chains = []
    for direction, (q_ref, k_ref, v_ref, gb_ref, o_ref) in enumerate(
            ((qf, kf, vf, gf, of_ref), (qb, kb, vb, gbk, ob_ref))):
        incl = (row >= col) if direction == 0 else (row <= col)
        strict = (row > col) if direction == 0 else (row < col)
        last = cc - 1 if direction == 0 else 0
        gb = gb_ref[...]
        gc_all = _split_dot(incl.astype(BF16), gb)
        gc_rows = jnp.concatenate([gc_all, jnp.zeros_like(gc_all)], axis=0).T if cc < HEAD_DIM else gc_all.T
        for h in range(H_A):
            lane_beta = direction * H_A + h
            lane_g = 2 * H_A + direction * H_A + h
            sl = slice(h * HEAD_DIM, (h + 1) * HEAD_DIM)
            chains.append(dict(
                q=q_ref[:, sl], k=k_ref[:, sl], v=v_ref[:, sl], o_ref=o_ref, sl=sl, idx=direction * H_A + h,
                beta=gb[:, lane_beta:lane_beta + 1],
                gcc=gc_all[:, lane_g:lane_g + 1],
                gcr=gc_rows[lane_g:lane_g + 1, 0:cc],
                incl=incl, strict=strict, last=last))

    for c in chains:
        c["decay"] = jnp.exp(jnp.where(c["incl"], c["gcc"] - c["gcr"], NEG_BIG))
        c["egc"] = jnp.exp(c["gcc"])
        c["kb"] = c["k"] * c["beta"]
        c["vb"] = c["v"] * c["beta"]
    nt = (((1,), (1,)), ((), ()))
    kks = [lax.dot_general(jnp.concatenate([c["kb"], c["q"]], axis=0).astype(BF16), c["k"].astype(BF16), nt,
                           preferred_element_type=F32) for c in chains]
    for c, kk in zip(chains, kks):
        c["a"] = jnp.where(c["strict"], kk[:cc] * c["decay"], 0.0)
        c["qk"] = kk[cc:] * c["decay"]
    ts = [eye - jnp.where(level_masks[0], c["a"], 0.0) for c in chains]
    for mask in level_masks[1:]:
        tb = [_bdot(t, jnp.where(mask, c["a"], 0.0)) for t, c in zip(ts, chains)]
        tbt = [_bdot(x, t) for x, t in zip(tb, ts)]
        ts = [t - x for t, x in zip(ts, tbt)]
    uws = [_bdot(t, jnp.concatenate([c["vb"], c["kb"] * c["egc"]], axis=1)) for t, c in zip(ts, chains)]
    s_olds = [s_ref[c["idx"]] for c in chains]
    wqs = [_bdot(jnp.concatenate([uw[:, HEAD_DIM:], c["q"] * c["egc"]], axis=0), s_old)
           for uw, c, s_old in zip(uws, chains, s_olds)]
    v_news = [uw[:, :HEAD_DIM] - wq[:cc] for uw, wq in zip(uws, wqs)]
    intra = [_bdot(c["qk"], v_new) for c, v_new in zip(chains, v_news)]
    for c, wq, x in zip(chains, wqs, intra):
        c["o_ref"][:, c["sl"]] = wq[cc:] + x
    tn = (((0,), (0,)), ((), ()))
    g_lasts = [c["gcc"][c["last"]:c["last"] + 1, :] for c in chains]
    kvs = [lax.dot_general((c["k"] * jnp.exp(g_last - c["gcc"])).astype(BF16), v_new.astype(BF16), tn,
                           preferred_element_type=F32) for c, g_last, v_new in zip(chains, g_lasts, v_news)]
    for c, s_old, g_last, kv in zip(chains, s_olds, g_lasts, kvs):
        s_ref[c["idx"]] = s_old * jnp.exp(g_last) + kv


def deltanet_recurrence(qkv, gb, *, chunk=DN_CHUNK):
    length = qkv.shape[0]
    n = length // chunk
    fwd = lambda j: (lambda c: (c, j))
    bwd = lambda j: (lambda c: (n - 1 - c, j))
    col = lambda m, width: pl.BlockSpec((chunk, width), m)
    return pl.pallas_call(
        functools.partial(_delta_kernel, chunk=chunk),
        grid=(n,),
        in_specs=[col(fwd(0), W_A), col(fwd(1), W_A), col(fwd(2), W_A), col(fwd(0), HEAD_DIM),
                  col(bwd(0), W_A), col(bwd(1), W_A), col(bwd(2), W_A), col(bwd(0), HEAD_DIM)],
        out_specs=[col(fwd(0), W_A), col(bwd(0), W_A)],
        out_shape=[jax.ShapeDtypeStruct((length, W_A), F32)] * 2,
        scratch_shapes=[pltpu.VMEM((2 * H_A, HEAD_DIM, HEAD_DIM), F32)],
        name="delta_recurrence",
        compiler_params=_params("arbitrary"),
    )(qkv, qkv, qkv, gb, qkv, qkv, qkv, gb)


def _assemble_kernel(of_ref, ob_ref, gate_ref, og_ref, o0, o1, o2, l0, l1, l2, yc_ref, mix_ref, buf_ref, *, tm):
    og = og_ref[...]
    for h in range(H_A):
        sl = slice(h * HEAD_DIM, (h + 1) * HEAD_DIM)
        o = of_ref[:, sl] + ob_ref[:, sl]
        ms = jnp.mean(o * o, axis=-1, keepdims=True)
        gate = gate_ref[:, sl]
        mix_ref[:, sl] = (o * lax.rsqrt(ms + EPS) * og * (gate * jax.nn.sigmoid(gate))).astype(mix_ref.dtype)

    def positions(ref, h, gi, slot):
        d = DILATED_PAIRS[gi][1]
        if d == 1:
            return ref[h]
        for r in range(d):
            buf_ref[slot, pl.ds(r, tm // d, stride=d), :] = ref[h, :, r * HEAD_DIM:(r + 1) * HEAD_DIM]
        return buf_ref[slot]

    outs, lses = (o0, o1, o2), (l0, l1, l2)
    n_groups = len(DILATED_PAIRS)
    for h in range(H_B_GROUP):
        ls = [positions(lses[gi], h, gi, (h * n_groups + gi) * 2) for gi in range(n_groups)]
        mx = jnp.maximum(jnp.maximum(ls[0], ls[1]), ls[2])
        es = [jnp.exp(t - mx) for t in ls]
        tot = es[0] + es[1] + es[2]
        for gi in range(n_groups):
            c0 = W_A + (gi * H_B_GROUP + h) * HEAD_DIM
            o = positions(outs[gi], h, gi, (h * n_groups + gi) * 2 + 1)
            mix_ref[:, c0:c0 + HEAD_DIM] = (o * (es[gi] / tot)).astype(mix_ref.dtype)
    mix_ref[:, W_A + W_B:] = yc_ref[...]


def assemble_mix(o_fwd, o_bwd, proj, onorm_g, dil_outs, dil_lses, y_c, *, tm=256):
    length = o_fwd.shape[0]
    row = lambda w: pl.BlockSpec((tm, w), lambda i: (i, 0))
    hblks = [pl.BlockSpec((H_B_GROUP, tm // d, d * HEAD_DIM), lambda i: (0, i, 0)) for _, d in DILATED_PAIRS]
    return pl.pallas_call(
        functools.partial(_assemble_kernel, tm=tm),
        grid=(length // tm,),
        in_specs=[row(W_A), row(W_A), pl.BlockSpec((tm, W_A), lambda i: (i, OFF_AGATE // W_A)),
                  pl.BlockSpec((1, HEAD_DIM), lambda i: (0, 0))] + hblks * 2 + [row(W_CQ)],
        out_specs=row(D_MODEL),
        out_shape=jax.ShapeDtypeStruct((length, D_MODEL), BF16),
        scratch_shapes=[pltpu.VMEM((2 * H_B, tm, HEAD_DIM), F32)],
        name="assemble_mix",
        compiler_params=_params("parallel"),
    )(o_fwd, o_bwd, proj, onorm_g.reshape(1, HEAD_DIM), *dil_outs, *dil_lses, y_c)


def _cross_kernel(x_ref, gx_ref, wq_ref, k_ref, v_ref, wo_ref, gf_ref, xo_ref, hf_ref):
    x = x_ref[...]
    ms = jnp.mean(x * x, axis=-1, keepdims=True)
    h = (x * lax.rsqrt(ms + EPS) * gx_ref[...]).astype(BF16)
    q = jnp.dot(h, wq_ref[...], preferred_element_type=F32)
    sls = [slice(hh * HEAD_DIM, (hh + 1) * HEAD_DIM) for hh in range(X_HEADS)]
    ss = [lax.dot_general(q[:, sl].astype(BF16), k_ref[:, sl], (((1,), (1,)), ((), ())),
                          preferred_element_type=F32) * SCALE for sl in sls]
    es = [jnp.exp(s - jnp.max(s, axis=-1, keepdims=True)) for s in ss]
    ps = [e / jnp.sum(e, axis=-1, keepdims=True) for e in es]
    heads = [jnp.dot(p.astype(BF16), v_ref[:, sl], preferred_element_type=F32).astype(BF16) for p, sl in zip(ps, sls)]
    o = jnp.concatenate(heads, axis=1)
    xn = x + jnp.dot(o, wo_ref[...], preferred_element_type=F32)
    xo_ref[...] = xn
    ms2 = jnp.mean(xn * xn, axis=-1, keepdims=True)
    hf_ref[...] = (xn * lax.rsqrt(ms2 + EPS) * gf_ref[...]).astype(hf_ref.dtype)


def cross_attention_block(x, g_x, wq, kv, wo, g_ffn, layer, *, tm=256):
    length = x.shape[0]
    width = X_HEADS * HEAD_DIM
    full = lambda shape: pl.BlockSpec(shape, lambda i: (0, 0))
    stacked = lambda shape: pl.BlockSpec((None,) + shape, lambda i: (layer, 0, 0))
    row = pl.BlockSpec((tm, D_MODEL), lambda i: (i, 0))
    return pl.pallas_call(
        _cross_kernel,
        grid=(length // tm,),
        in_specs=[row, full((1, D_MODEL)), stacked((D_MODEL, width)),
                  pl.BlockSpec((N_MEM, width), lambda i: (0, 0)), pl.BlockSpec((N_MEM, width), lambda i: (0, 1)),
                  stacked((width, D_MODEL)), full((1, D_MODEL))],
        out_specs=[row, row],
        out_shape=[jax.ShapeDtypeStruct((length, D_MODEL), F32), jax.ShapeDtypeStruct((length, D_MODEL), BF16)],
        name="cross_attn",
        compiler_params=_params("parallel"),
    )(x, g_x.reshape(1, D_MODEL), wq, kv, kv, wo, g_ffn.reshape(1, D_MODEL))


def _permute_w_in(w):
    sizes = (3 * W_A, W_A, 2 * H_A, 2 * H_A, 3 * W_B, W_CQ, W_CKV, W_CKV)
    offs = [0]
    for s in sizes:
        offs.append(offs[-1] + s)
    a_qkv, a_gate, a_beta, a_decay, b_qkv, c_q, c_k, c_v = (w[..., offs[i]:offs[i + 1]].astype(BF16) for i in range(8))
    pad = jnp.zeros(w.shape[:-1] + (W_SMALL - 4 * H_A,), BF16)
    return jnp.concatenate([c_q, b_qkv, c_k, c_v, a_qkv, a_gate, a_beta, a_decay, pad], axis=-1)


def _encode(x, mem, tables_b, tables_c, p):
    dil_groups = tuple((H_B_GROUP, d) for _, d in DILATED_PAIRS)
    for l in range(p["w_in"].shape[0]):
        h = rmsnorm_rows(x, p["norm_mix_g"][l], BF16)
        proj = matmul(h, p["w_in"], l, tm=1024, tn=512)
        qkv_a = deltanet_conv_prep(proj, p["conv_w"][l])
        gb = deltanet_gate_prep(proj, p["a_log"][l], p["dt_bias"][l])
        o_fwd, o_bwd = deltanet_recurrence(qkv_a, gb)
        q_b = head_prep(proj, OFF_BQ, dil_groups, tables=tables_b, half=ROT_DIMS // 2, scale=SCALE)
        k_b = head_prep(proj, OFF_BK, dil_groups, tables=tables_b, half=ROT_DIMS // 2)
        v_b = head_prep(proj, OFF_BV, dil_groups)
        dil = [dilated_group_attention(q_b[gi], k_b[gi], v_b[gi], w, d) for gi, (w, d) in enumerate(DILATED_PAIRS)]
        q_c, = head_prep(proj, OFF_CQ, ((H_C, 1),), gain=p["qnorm_g"][l], tables=tables_c, half=HEAD_DIM // 4,
                         scale=SCALE * LOG2_E)
        k_c, = head_prep(proj, OFF_CK, ((KV_C, 1),), gain=p["knorm_g"][l], tables=tables_c, half=HEAD_DIM // 4)
        v_c, = head_prep(proj, OFF_CV, ((KV_C, 1),))
        y_c = axial_flash_attention(q_c, k_c, v_c)
        mix = assemble_mix(o_fwd, o_bwd, proj, p["onorm_g"][l], [t[0] for t in dil], [t[1] for t in dil], y_c)
        x = matmul(mix, p["w_out"], l, tm=1024, tn=512, residual=x)
        mem_n = rmsnorm_rows(mem, p["norm_mem_g"][l], BF16)
        kv = matmul(mem_n, p["wkv_x"], l, tm=N_MEM, tn=512, out_dtype=BF16)
        x, hf = cross_attention_block(x, p["norm_x_g"][l], p["wq_x"], kv, p["wo_x"], p["norm_ffn_g"][l], l)
        up = matmul(hf, p["w_up"], l, tm=1024, tn=1024, relu2=True, out_dtype=BF16)
        x = matmul(up, p["w_down"], l, tm=1024, tn=1024, tk=2048, residual=x)
    return rmsnorm_rows(x, p["norm_final_g"], F32)


def _tables(length):
    pos = jnp.arange(length, dtype=F32)
    tables_b = _rope_tables([pos], ROPE_THETA, ROT_DIMS)
    rows = length // GRID_W
    row_pos = jnp.repeat(jnp.arange(rows, dtype=F32), GRID_W)
    col_pos = jnp.tile(jnp.arange(GRID_W, dtype=F32), rows)
    tables_c = _rope_tables([row_pos, col_pos], AXIAL_THETA, HEAD_DIM // 2)
    return tables_b, tables_c


def kernel(x_prompt, x_sample, mem_prompt, mem_sample, norm_mix_g, w_in, conv_w, a_log, dt_bias, onorm_g, qnorm_g, knorm_g, w_out, norm_x_g, norm_mem_g, wq_x, wk_x, wv_x, wo_x, norm_ffn_g, w_up, w_down, norm_final_g):
    p = dict(
        norm_mix_g=norm_mix_g, conv_w=conv_w, a_log=a_log, dt_bias=dt_bias, onorm_g=onorm_g, qnorm_g=qnorm_g,
        knorm_g=knorm_g, norm_x_g=norm_x_g, norm_mem_g=norm_mem_g, norm_ffn_g=norm_ffn_g, norm_final_g=norm_final_g,
        w_in=_permute_w_in(w_in),
        w_out=w_out.astype(BF16), wq_x=wq_x.astype(BF16),
        wkv_x=jnp.concatenate([wk_x, wv_x], axis=-1).astype(BF16),
        wo_x=wo_x.astype(BF16), w_up=w_up.astype(BF16), w_down=w_down.astype(BF16),
    )
    outs = []
    for x, mem in ((x_prompt, mem_prompt), (x_sample, mem_sample)):
        length = x.shape[1]
        tables_b, tables_c = _tables(length)
        y = _encode(x[0], mem[0], tables_b, tables_c, p)
        outs.append(y[None])
    return tuple(outs)
```

```python
import functools
import math

import jax
import jax.numpy as jnp
from jax import lax
from jax.experimental import pallas as pl
from jax.experimental.pallas import tpu as pltpu

F32 = jnp.float32
BF16 = jnp.bfloat16

D_MODEL = 4096
HEAD_DIM = 128
H_A = 8
DILATED_PAIRS = ((128, 1), (512, 4), (2048, 16))
H_B_GROUP = 4
H_B = 12
H_C = 12
KV_C = 4
GQA_G = H_C // KV_C
CONV_W = 4
ROPE_THETA = 500000.0
ROT_DIMS = HEAD_DIM // 4
AXIAL_THETA = 10000.0
GRID_W = 64
N_MEM = 256
X_HEADS = 4
D_FF = 4 * D_MODEL
EPS = 1e-6
SCALE = HEAD_DIM ** -0.5
LOG2_E = math.log2(math.e)
NEG_BIG = -1e30

W_A = H_A * HEAD_DIM
W_B = H_B * HEAD_DIM
W_CQ = H_C * HEAD_DIM
W_CKV = KV_C * HEAD_DIM
COLS_A = 4 * W_A
COLS_S = 4 * H_A
COLS_BC = 3 * W_B + W_CQ + 2 * W_CKV
OFF_AGATE = 3 * W_A
OFF_BQ = 0
OFF_BK = OFF_BQ + W_B
OFF_BV = OFF_BK + W_B
OFF_CQ = OFF_BV + W_B
OFF_CK = OFF_CQ + W_CQ
OFF_CV = OFF_CK + W_CKV

V7X_VMEM_LIMIT_BYTES = 56 * 1024 * 1024
DN_CHUNK = 64


def _params(*sem):
    return pltpu.CompilerParams(dimension_semantics=sem, vmem_limit_bytes=V7X_VMEM_LIMIT_BYTES)


def _rmsnorm_kernel(x_ref, g_ref, o_ref):
    x = x_ref[...]
    ms = jnp.mean(x * x, axis=-1, keepdims=True)
    o_ref[...] = (x * lax.rsqrt(ms + EPS) * g_ref[...]).astype(o_ref.dtype)


def rmsnorm_rows(x, g, out_dtype, tm=256):
    m, d = x.shape
    return pl.pallas_call(
        _rmsnorm_kernel,
        grid=(m // tm,),
        in_specs=[pl.BlockSpec((tm, d), lambda i: (i, 0)), pl.BlockSpec((1, d), lambda i: (0, 0))],
        out_specs=pl.BlockSpec((tm, d), lambda i: (i, 0)),
        out_shape=jax.ShapeDtypeStruct((m, d), out_dtype),
        name="rmsnorm_rows",
        compiler_params=_params("parallel"),
    )(x, g.reshape(1, d))


def _mm_kernel(*refs, nk, relu2, has_res, use_scratch):
    a_ref, w_ref = refs[0], refs[1]
    r_ref = refs[2] if has_res else None
    o_ref = refs[3] if has_res else refs[2]
    def part():
        return jnp.dot(a_ref[...], w_ref[...], preferred_element_type=F32)

    def finish(acc):
        if relu2:
            acc = jnp.square(jnp.maximum(acc, 0.0))
        return acc.astype(o_ref.dtype)

    if nk == 1:
        o_ref[...] = finish(r_ref[...] + part() if has_res else part())
        return
    acc_ref = refs[-1] if use_scratch else o_ref
    k = pl.program_id(2)

    @pl.when(k == 0)
    def _():
        acc_ref[...] = (r_ref[...] + part()) if has_res else part()

    @pl.when(k > 0)
    def _():
        acc_ref[...] = acc_ref[...] + part()

    if use_scratch or relu2:
        @pl.when(k == nk - 1)
        def _():
            o_ref[...] = finish(acc_ref[...])


def matmul(a, w, layer, *, tm, tn, tk=None, residual=None, relu2=False, out_dtype=F32):
    m, kd = a.shape
    n = w.shape[2]
    tm = min(tm, m)
    tk = kd if tk is None else tk
    nk = kd // tk
    has_res = residual is not None
    use_scratch = nk > 1 and out_dtype != F32
    in_specs = [pl.BlockSpec((tm, tk), lambda i, j, k: (i, k)),
                pl.BlockSpec((None, tk, tn), lambda i, j, k: (layer, k, j))]
    args = [a, w]
    if has_res:
        in_specs.append(pl.BlockSpec((tm, tn), lambda i, j, k: (i, j)))
        args.append(residual)
    return pl.pallas_call(
        functools.partial(_mm_kernel, nk=nk, relu2=relu2, has_res=has_res, use_scratch=use_scratch),
        grid=(m // tm, n // tn, nk),
        in_specs=in_specs,
        out_specs=pl.BlockSpec((tm, tn), lambda i, j, k: (i, j)),
        out_shape=jax.ShapeDtypeStruct((m, n), out_dtype),
        scratch_shapes=[pltpu.VMEM((tm, tn), F32)] if use_scratch else [],
        name="matmul",
        compiler_params=_params("parallel", "parallel", "arbitrary"),
    )(*args)


def _rope_tables(pos_list, theta, width_each):
    half = width_each // 2
    inv = theta ** (-jnp.arange(half, dtype=F32) / half)
    cos_parts, sin_parts = [], []
    for pos in pos_list:
        ang = pos[:, None] * inv[None, :]
        c, s = jnp.cos(ang), jnp.sin(ang)
        cos_parts += [c, c]
        sin_parts += [-s, s]
    length = pos_list[0].shape[0]
    rest = HEAD_DIM - width_each * len(pos_list)
    if rest:
        cos_parts.append(jnp.ones((length, rest), F32))
        sin_parts.append(jnp.zeros((length, rest), F32))
    return jnp.concatenate(cos_parts, axis=1), jnp.concatenate(sin_parts, axis=1)


def _head_prep_kernel(*refs, groups, tm, half, use_norm, use_rope, scale, ones_pad):
    x_ref = refs[0]
    n_out = len(groups)
    o_refs = refs[-n_out - 1:-1]
    buf_ref = refs[-1]
    idx = 1
    if use_norm:
        g = refs[idx][...]
        idx += 1
    if use_rope:
        cos = refs[idx][...]
        sin = refs[idx + 1][...]
        lane = lax.broadcasted_iota(jnp.int32, cos.shape, 1)
        first = (lane % (2 * half)) < half
    h = 0
    for o_ref, (n_heads, dilation) in zip(o_refs, groups):
        for hg in range(n_heads):
            x = x_ref[:, h * HEAD_DIM:(h + 1) * HEAD_DIM]
            if use_norm:
                ms = jnp.mean(x * x, axis=-1, keepdims=True)
                x = x * lax.rsqrt(ms + EPS) * g
            if use_rope:
                partner = jnp.where(first, pltpu.roll(x, HEAD_DIM - half, axis=1), pltpu.roll(x, half, axis=1))
                x = x * cos + partner * sin
            if scale != 1.0:
                x = x * scale
            if ones_pad:
                o_ref[hg, :, :HEAD_DIM] = x.astype(o_ref.dtype)
                o_ref[hg, :, HEAD_DIM:] = jnp.ones((tm, HEAD_DIM), o_ref.dtype)
            elif dilation == 1:
                o_ref[hg] = x.astype(o_ref.dtype)
            else:
                buf_ref[h] = x
                for r in range(dilation):
                    o_ref[hg, :, r * HEAD_DIM:(r + 1) * HEAD_DIM] = buf_ref[
                        h, pl.ds(r, tm // dilation, stride=dilation), :].astype(o_ref.dtype)
            h += 1


def head_prep(proj, col_off, groups, *, gain=None, tables=None, half=None, scale=1.0, ones_pad=False, tm=512):
    length = proj.shape[0]
    n_heads = sum(n for n, _ in groups)
    width = n_heads * HEAD_DIM
    in_specs = [pl.BlockSpec((tm, width), lambda i: (i, col_off // width))]
    args = [proj]
    if gain is not None:
        in_specs.append(pl.BlockSpec((1, HEAD_DIM), lambda i: (0, 0)))
        args.append(gain.reshape(1, HEAD_DIM))
    if tables is not None:
        in_specs += [pl.BlockSpec((tm, HEAD_DIM), lambda i: (i, 0))] * 2
        args += list(tables)
    lanes = 2 if ones_pad else 1
    outs = pl.pallas_call(
        functools.partial(_head_prep_kernel, groups=groups, tm=tm, half=half, use_norm=gain is not None,
                          use_rope=tables is not None, scale=scale, ones_pad=ones_pad),
        grid=(length // tm,),
        in_specs=in_specs,
        out_specs=[pl.BlockSpec((n, tm // d, lanes * d * HEAD_DIM), lambda i: (0, i, 0)) for n, d in groups],
        out_shape=[jax.ShapeDtypeStruct((n, length // d, lanes * d * HEAD_DIM), BF16) for n, d in groups],
        scratch_shapes=[pltpu.VMEM((n_heads, tm, HEAD_DIM), F32)],
        name="head_prep",
        compiler_params=_params("parallel"),
    )(*args)
    return outs


def _flash_kernel(q_ref, k_ref, v_ref, o_ref, m_ref, acc_ref, *, tq, nkv, chunk, nchunk):
    ki = pl.program_id(2)

    @pl.when(ki == 0)
    def _():
        m_ref[...] = jnp.full(m_ref.shape, NEG_BIG, F32)
        acc_ref[...] = jnp.zeros(acc_ref.shape, F32)

    q = q_ref[...].reshape(GQA_G * tq, HEAD_DIM)
    ntile = chunk // HEAD_DIM

    def scores(j):
        return lax.dot_general(q, k_ref[0, j * chunk:(j + 1) * chunk, :], (((1,), (1,)), ((), ())),
                               preferred_element_type=F32)

    heads = range(GQA_G)
    rows = [slice(g * tq, (g + 1) * tq) for g in heads]
    m = [m_ref[r, :] for r in rows]
    acc = [acc_ref[r, :] for r in rows]
    s_next = scores(0)
    for j in range(nchunk):
        s = s_next
        if j + 1 < nchunk:
            s_next = scores(j + 1)
        alphas, p_rows = [], []
        for g in heads:
            tiles = [s[rows[g], t * HEAD_DIM:(t + 1) * HEAD_DIM] for t in range(ntile)]
            part = tiles[0]
            for t in tiles[1:]:
                part = jnp.maximum(part, t)
            m_new = jnp.maximum(m[g], jnp.max(part, axis=-1, keepdims=True))
            alpha = jnp.exp2(m[g] - m_new)
            alphas.append(jnp.concatenate([alpha, alpha], axis=1))
            p_rows.append(jnp.concatenate([jnp.exp2(t - m_new).astype(BF16) for t in tiles], axis=1))
            m[g] = m_new
        pv = jnp.dot(jnp.concatenate(p_rows, axis=0), v_ref[0, j * chunk:(j + 1) * chunk, :],
                     preferred_element_type=F32)
        for g in heads:
            acc[g] = alphas[g] * acc[g] + pv[rows[g]]
    for g in heads:
        m_ref[rows[g], :], acc_ref[rows[g], :] = m[g], acc[g]

    @pl.when(ki == nkv - 1)
    def _():
        for g in heads:
            o = acc[g][:, :HEAD_DIM] / acc[g][:, HEAD_DIM:]
            o_ref[:, g * HEAD_DIM:(g + 1) * HEAD_DIM] = o.astype(o_ref.dtype)


def axial_flash_attention(q, k, v, *, tq=256, tk=8192, chunk=512):
    length = q.shape[1]
    tk = min(tk, length)
    chunk = min(chunk, tk)
    nkv = length // tk
    rows = GQA_G * tq
    return pl.pallas_call(
        functools.partial(_flash_kernel, tq=tq, nkv=nkv, chunk=chunk, nchunk=tk // chunk),
        grid=(KV_C, length // tq, nkv),
        in_specs=[pl.BlockSpec((GQA_G, tq, HEAD_DIM), lambda h, i, j: (h, i, 0)),
                  pl.BlockSpec((1, tk, HEAD_DIM), lambda h, i, j: (h, j, 0)),
                  pl.BlockSpec((1, tk, 2 * HEAD_DIM), lambda h, i, j: (h, j, 0))],
        out_specs=pl.BlockSpec((tq, GQA_G * HEAD_DIM), lambda h, i, j: (i, h)),
        out_shape=jax.ShapeDtypeStruct((length, H_C * HEAD_DIM), BF16),
        scratch_shapes=[pltpu.VMEM((rows, HEAD_DIM), F32), pltpu.VMEM((rows, 2 * HEAD_DIM), F32)],
        name="axial_flash",
        compiler_params=_params("parallel", "parallel", "arbitrary"),
    )(q, k, v)


def _dilated_kernel(q_ref, kp_ref, kc_ref, kn_ref, vp_ref, vc_ref, vn_ref, o_ref, lse_ref, *, tq, side, n_sub):
    c = pl.program_id(1)
    row = lax.broadcasted_iota(jnp.int32, (tq, tq + 2 * side), 0)
    col = lax.broadcasted_iota(jnp.int32, (tq, tq + 2 * side), 1)
    key_idx = c * tq - side + col
    valid = (jnp.abs(row + side - col) <= side) & (key_idx >= 0) & (key_idx < n_sub)
    heads = range(H_B_GROUP)
    halo = lambda p_ref, c_ref, n_ref, h: jnp.concatenate([p_ref[h, tq - side:, :], c_ref[h], n_ref[h, :side, :]], axis=0)
    ss = [lax.dot_general(q_ref[h], halo(kp_ref, kc_ref, kn_ref, h), (((1,), (1,)), ((), ())),
                          preferred_element_type=F32) for h in heads]
    ss = [jnp.where(valid, s, NEG_BIG) for s in ss]
    ms = [jnp.max(s, axis=-1, keepdims=True) for s in ss]
    es = [jnp.exp(s - m) for s, m in zip(ss, ms)]
    dens = [jnp.sum(e, axis=-1, keepdims=True) for e in es]
    os_ = [jnp.dot(e.astype(BF16), halo(vp_ref, vc_ref, vn_ref, h), preferred_element_type=F32)
           for e, h in zip(es, heads)]
    for h in heads:
        o_ref[h] = os_[h] / dens[h]
        lse_ref[h] = jnp.broadcast_to(ms[h] + jnp.log(dens[h]), (tq, HEAD_DIM))


def dilated_group_attention(q, k, v, window, dilation, *, tq=256):
    d = dilation
    n_sub = q.shape[1]
    side = (window // 2) // d
    nblk = n_sub // tq
    cur = lambda r, c: (0, c, r)
    prev = lambda r, c: (0, jnp.maximum(c - 1, 0), r)
    nxt = lambda r, c: (0, jnp.minimum(c + 1, nblk - 1), r)
    blk = (H_B_GROUP, tq, HEAD_DIM)
    return pl.pallas_call(
        functools.partial(_dilated_kernel, tq=tq, side=side, n_sub=n_sub),
        grid=(d, nblk),
        in_specs=[pl.BlockSpec(blk, cur), pl.BlockSpec(blk, prev), pl.BlockSpec(blk, cur), pl.BlockSpec(blk, nxt),
                  pl.BlockSpec(blk, prev), pl.BlockSpec(blk, cur), pl.BlockSpec(blk, nxt)],
        out_specs=[pl.BlockSpec(blk, cur)] * 2,
        out_shape=[jax.ShapeDtypeStruct((H_B_GROUP, n_sub, d * HEAD_DIM), F32)] * 2,
        name="dilated_attn",
        compiler_params=_params("parallel", "parallel"),
    )(q, k, k, k, v, v, v)


def _conv_prep_kernel(cur_ref, prev_ref, next_ref, w_ref, o_ref, *, tm, nblk):
    i = pl.program_id(0)
    j = pl.program_id(1)
    x = cur_ref[...]
    before = jnp.where(i > 0, prev_ref[7:8, :], 0.0)
    after = jnp.where(i < nblk - 1, next_ref[0:2, :], 0.0)
    row = lax.broadcasted_iota(jnp.int32, x.shape, 0)
    xm1 = jnp.where(row == 0, before, pltpu.roll(x, 1, axis=0))
    xp1 = jnp.where(row == tm - 1, after[0:1, :], pltpu.roll(x, tm - 1, axis=0))
    xp2 = jnp.where(row == tm - 2, after[0:1, :],
                    jnp.where(row == tm - 1, after[1:2, :], pltpu.roll(x, tm - 2, axis=0)))
    w = w_ref[...]
    y = xm1 * w[0:1, :] + x * w[1:2, :] + xp1 * w[2:3, :] + xp2 * w[3:4, :]
    y = y * jax.nn.sigmoid(y)
    post = jnp.where(j == 0, SCALE, 1.0)
    for h in range(H_A):
        yh = y[:, h * HEAD_DIM:(h + 1) * HEAD_DIM]
        inv = lax.rsqrt(jnp.sum(yh * yh, axis=-1, keepdims=True) + EPS) * post
        o_ref[:, h * HEAD_DIM:(h + 1) * HEAD_DIM] = yh * jnp.where(j < 2, inv, 1.0)


def deltanet_conv_prep(proj, conv_w, *, tm=256):
    length = proj.shape[0]
    nblk = length // tm
    cb = 0
    hb = tm // 8
    return pl.pallas_call(
        functools.partial(_conv_prep_kernel, tm=tm, nblk=nblk),
        grid=(nblk, 3),
        in_specs=[pl.BlockSpec((tm, W_A), lambda i, j: (i, cb + j)),
                  pl.BlockSpec((8, W_A), lambda i, j: (jnp.maximum(i * hb - 1, 0), cb + j)),
                  pl.BlockSpec((8, W_A), lambda i, j: (jnp.minimum((i + 1) * hb, nblk * hb - 1), cb + j)),
                  pl.BlockSpec((CONV_W, W_A), lambda i, j: (0, j))],
        out_specs=pl.BlockSpec((tm, W_A), lambda i, j: (i, j)),
        out_shape=jax.ShapeDtypeStruct((length, 3 * W_A), F32),
        name="delta_conv_prep",
        compiler_params=_params("parallel", "parallel"),
    )(proj, proj, proj, conv_w)


def _gate_prep_kernel(x_ref, a_ref, dt_ref, o_ref):
    x = x_ref[...]
    lane = lax.broadcasted_iota(jnp.int32, x.shape, 1)
    beta = jax.nn.sigmoid(x)
    z = x + dt_ref[...]
    softplus = jnp.maximum(z, 0.0) + jnp.log(1.0 + jnp.exp(-jnp.abs(z)))
    g = -jnp.exp(a_ref[...]) * softplus
    o_ref[...] = jnp.where(lane < 2 * H_A, beta, jnp.where(lane < 4 * H_A, g, 0.0))


def deltanet_gate_prep(proj, a_log, dt_bias, *, tm=512):
    length = proj.shape[0]
    pad = lambda t: jnp.zeros((1, HEAD_DIM), F32).at[0, 2 * H_A:4 * H_A].set(t.reshape(-1))
    return pl.pallas_call(
        _gate_prep_kernel,
        grid=(length // tm,),
        in_specs=[pl.BlockSpec((tm, HEAD_DIM), lambda i: (i, 0)),
                  pl.BlockSpec((1, HEAD_DIM), lambda i: (0, 0)),
                  pl.BlockSpec((1, HEAD_DIM), lambda i: (0, 0))],
        out_specs=pl.BlockSpec((tm, HEAD_DIM), lambda i: (i, 0)),
        out_shape=jax.ShapeDtypeStruct((length, HEAD_DIM), F32),
        name="delta_gate_prep",
        compiler_params=_params("parallel"),
    )(proj, pad(a_log), pad(dt_bias))


def _bdot(a, b):
    return jnp.dot(a.astype(BF16), b.astype(BF16), preferred_element_type=F32)


def _split_dot(tri, x):
    hi = x.astype(BF16)
    lo = (x - hi.astype(F32)).astype(BF16)
    return jnp.dot(tri, hi, preferred_element_type=F32) + jnp.dot(tri, lo, preferred_element_type=F32)


def _delta_kernel(qf, kf, vf, gf, qb, kb, vb, gbk, of_ref, ob_ref, s_ref, *, chunk):
    @pl.when(pl.program_id(0) == 0)
    def _():
        s_ref[...] = jnp.zeros(s_ref.shape, F32)

    cc = chunk
    row = lax.broadcasted_iota(jnp.int32, (cc, cc), 0)
    col = lax.broadcasted_iota(jnp.int32, (cc, cc), 1)
    eye = (row == col).astype(F32)
    level_masks = []
    differ = row ^ col
    b = 2
    while b <= cc:
        level_masks.append((differ < b) & (differ >= b // 2))
        b *= 2

    chains = []
    for direction, (q_ref, k_ref, v_ref, gb_ref, o_ref) in enumerate(
            ((qf, kf, vf, gf, of_ref), (qb, kb, vb, gbk, ob_ref))):
        incl = (row >= col) if direction == 0 else (row <= col)
        strict = (row > col) if direction == 0 else (row < col)
        last = cc - 1 if direction == 0 else 0
        gb = gb_ref[...]
        gc_all = _split_dot(incl.astype(BF16), gb)
        gc_rows = jnp.concatenate([gc_all, jnp.zeros_like(gc_all)], axis=0).T if cc < HEAD_DIM else gc_all.T
        for h in range(H_A):
            lane_beta = direction * H_A + h
            lane_g = 2 * H_A + direction * H_A + h
            sl = slice(h * HEAD_DIM, (h + 1) * HEAD_DIM)
            chains.append(dict(
                q=q_ref[:, sl], k=k_ref[:, sl], v=v_ref[:, sl], o_ref=o_ref, sl=sl, idx=direction * H_A + h,
                beta=gb[:, lane_beta:lane_beta + 1],
                gcc=gc_all[:, lane_g:lane_g + 1],
                gcr=gc_rows[lane_g:lane_g + 1, 0:cc],
                incl=incl, strict=strict, last=last))

    for c in chains:
        c["decay"] = jnp.exp(jnp.where(c["incl"], c["gcc"] - c["gcr"], NEG_BIG))
        c["egc"] = jnp.exp(c["gcc"])
        c["kb"] = c["k"] * c["beta"]
        c["vb"] = c["v"] * c["beta"]
    nt = (((1,), (1,)), ((), ()))
    kks = [lax.dot_general(jnp.concatenate([c["kb"], c["q"]], axis=0).astype(BF16), c["k"].astype(BF16), nt,
                           preferred_element_type=F32) for c in chains]
    for c, kk in zip(chains, kks):
        c["a"] = jnp.where(c["strict"], kk[:cc] * c["decay"], 0.0)
        c["qk"] = kk[cc:] * c["decay"]
    ts = [eye - jnp.where(level_masks[0], c["a"], 0.0) for c in chains]
    for mask in level_masks[1:]:
        tb = [_bdot(t, jnp.where(mask, c["a"], 0.0)) for t, c in zip(ts, chains)]
        tbt = [_bdot(x, t) for x, t in zip(tb, ts)]
        ts = [t - x for t, x in zip(ts, tbt)]
    uws = [_bdot(t, jnp.concatenate([c["vb"], c["kb"] * c["egc"]], axis=1)) for t, c in zip(ts, chains)]
    s_olds = [s_ref[c["idx"]] for c in chains]
    wqs = [_bdot(jnp.concatenate([uw[:, HEAD_DIM:], c["q"] * c["egc"]], axis=0), s_old)
           for uw, c, s_old in zip(uws, chains, s_olds)]
    v_news = [uw[:, :HEAD_DIM] - wq[:cc] for uw, wq in zip(uws, wqs)]
    intra = [_bdot(c["qk"], v_new) for c, v_new in zip(chains, v_news)]
    for c, wq, x in zip(chains, wqs, intra):
        c["o_ref"][:, c["sl"]] = wq[cc:] + x
    tn = (((0,), (0,)), ((), ()))
    g_lasts = [c["gcc"][c["last"]:c["last"] + 1, :] for c in chains]
    kvs = [lax.dot_general((c["k"] * jnp.exp(g_last - c["gcc"])).astype(BF16), v_new.astype(BF16), tn,
                           preferred_element_type=F32) for c, g_last, v_new in zip(chains, g_lasts, v_news)]
    for c, s_old, g_last, kv in zip(chains, s_olds, g_lasts, kvs):
        s_ref[c["idx"]] = s_old * jnp.exp(g_last) + kv


def deltanet_recurrence(qkv, gb, *, chunk=DN_CHUNK):
    length = qkv.shape[0]
    n = length // chunk
    fwd = lambda j: (lambda c: (c, j))
    bwd = lambda j: (lambda c: (n - 1 - c, j))
    col = lambda m, width: pl.BlockSpec((chunk, width), m)
    return pl.pallas_call(
        functools.partial(_delta_kernel, chunk=chunk),
        grid=(n,),
        in_specs=[col(fwd(0), W_A), col(fwd(1), W_A), col(fwd(2), W_A), col(fwd(0), HEAD_DIM),
                  col(bwd(0), W_A), col(bwd(1), W_A), col(bwd(2), W_A), col(bwd(0), HEAD_DIM)],
        out_specs=[col(fwd(0), W_A), col(bwd(0), W_A)],
        out_shape=[jax.ShapeDtypeStruct((length, W_A), F32)] * 2,
        scratch_shapes=[pltpu.VMEM((2 * H_A, HEAD_DIM, HEAD_DIM), F32)],
        name="delta_recurrence",
        compiler_params=_params("arbitrary"),
    )(qkv, qkv, qkv, gb, qkv, qkv, qkv, gb)


def _assemble_kernel(of_ref, ob_ref, gate_ref, og_ref, o0, o1, o2, l0, l1, l2, yc_ref, mix_ref, buf_ref, *, tm):
    og = og_ref[...]
    for h in range(H_A):
        sl = slice(h * HEAD_DIM, (h + 1) * HEAD_DIM)
        o = of_ref[:, sl] + ob_ref[:, sl]
        ms = jnp.mean(o * o, axis=-1, keepdims=True)
        gate = gate_ref[:, sl]
        mix_ref[:, sl] = (o * lax.rsqrt(ms + EPS) * og * (gate * jax.nn.sigmoid(gate))).astype(mix_ref.dtype)

    def positions(ref, h, gi, slot):
        d = DILATED_PAIRS[gi][1]
        if d == 1:
            return ref[h]
        for r in range(d):
            buf_ref[slot, pl.ds(r, tm // d, stride=d), :] = ref[h, :, r * HEAD_DIM:(r + 1) * HEAD_DIM]
        return buf_ref[slot]

    outs, lses = (o0, o1, o2), (l0, l1, l2)
    n_groups = len(DILATED_PAIRS)
    for h in range(H_B_GROUP):
        ls = [positions(lses[gi], h, gi, (h * n_groups + gi) * 2) for gi in range(n_groups)]
        mx = jnp.maximum(jnp.maximum(ls[0], ls[1]), ls[2])
        es = [jnp.exp(t - mx) for t in ls]
        tot = es[0] + es[1] + es[2]
        for gi in range(n_groups):
            c0 = W_A + (gi * H_B_GROUP + h) * HEAD_DIM
            o = positions(outs[gi], h, gi, (h * n_groups + gi) * 2 + 1)
            mix_ref[:, c0:c0 + HEAD_DIM] = (o * (es[gi] / tot)).astype(mix_ref.dtype)
    mix_ref[:, W_A + W_B:] = yc_ref[...]


def assemble_mix(o_fwd, o_bwd, proj, onorm_g, dil_outs, dil_lses, y_c, *, tm=256):
    length = o_fwd.shape[0]
    row = lambda w: pl.BlockSpec((tm, w), lambda i: (i, 0))
    hblks = [pl.BlockSpec((H_B_GROUP, tm // d, d * HEAD_DIM), lambda i: (0, i, 0)) for _, d in DILATED_PAIRS]
    return pl.pallas_call(
        functools.partial(_assemble_kernel, tm=tm),
        grid=(length // tm,),
        in_specs=[row(W_A), row(W_A), pl.BlockSpec((tm, W_A), lambda i: (i, OFF_AGATE // W_A)),
                  pl.BlockSpec((1, HEAD_DIM), lambda i: (0, 0))] + hblks * 2 + [row(W_CQ)],
        out_specs=row(D_MODEL),
        out_shape=jax.ShapeDtypeStruct((length, D_MODEL), BF16),
        scratch_shapes=[pltpu.VMEM((2 * H_B, tm, HEAD_DIM), F32)],
        name="assemble_mix",
        compiler_params=_params("parallel"),
    )(o_fwd, o_bwd, proj, onorm_g.reshape(1, HEAD_DIM), *dil_outs, *dil_lses, y_c)


def _cross_kernel(x_ref, gx_ref, wq_ref, k_ref, v_ref, wo_ref, gf_ref, xo_ref, hf_ref):
    x = x_ref[...]
    ms = jnp.mean(x * x, axis=-1, keepdims=True)
    h = (x * lax.rsqrt(ms + EPS) * gx_ref[...]).astype(BF16)
    q = jnp.dot(h, wq_ref[...], preferred_element_type=F32)
    sls = [slice(hh * HEAD_DIM, (hh + 1) * HEAD_DIM) for hh in range(X_HEADS)]
    ss = [lax.dot_general(q[:, sl].astype(BF16), k_ref[:, sl], (((1,), (1,)), ((), ())),
                          preferred_element_type=F32) * SCALE for sl in sls]
    es = [jnp.exp(s - jnp.max(s, axis=-1, keepdims=True)) for s in ss]
    ps = [e / jnp.sum(e, axis=-1, keepdims=True) for e in es]
    heads = [jnp.dot(p.astype(BF16), v_ref[:, sl], preferred_element_type=F32).astype(BF16) for p, sl in zip(ps, sls)]
    o = jnp.concatenate(heads, axis=1)
    xn = x + jnp.dot(o, wo_ref[...], preferred_element_type=F32)
    xo_ref[...] = xn
    ms2 = jnp.mean(xn * xn, axis=-1, keepdims=True)
    hf_ref[...] = (xn * lax.rsqrt(ms2 + EPS) * gf_ref[...]).astype(hf_ref.dtype)


def cross_attention_block(x, g_x, wq, kv, wo, g_ffn, layer, *, tm=256):
    length = x.shape[0]
    width = X_HEADS * HEAD_DIM
    full = lambda shape: pl.BlockSpec(shape, lambda i: (0, 0))
    stacked = lambda shape: pl.BlockSpec((None,) + shape, lambda i: (layer, 0, 0))
    row = pl.BlockSpec((tm, D_MODEL), lambda i: (i, 0))
    return pl.pallas_call(
        _cross_kernel,
        grid=(length // tm,),
        in_specs=[row, full((1, D_MODEL)), stacked((D_MODEL, width)),
                  pl.BlockSpec((N_MEM, width), lambda i: (0, 0)), pl.BlockSpec((N_MEM, width), lambda i: (0, 1)),
                  stacked((width, D_MODEL)), full((1, D_MODEL))],
        out_specs=[row, row],
        out_shape=[jax.ShapeDtypeStruct((length, D_MODEL), F32), jax.ShapeDtypeStruct((length, D_MODEL), BF16)],
        name="cross_attn",
        compiler_params=_params("parallel"),
    )(x, g_x.reshape(1, D_MODEL), wq, kv, kv, wo, g_ffn.reshape(1, D_MODEL))


def _split_w_in(w):
    w_a = w[..., :COLS_A].astype(BF16)
    w_s = jnp.pad(w[..., COLS_A:COLS_A + COLS_S].astype(BF16), ((0, 0), (0, 0), (0, HEAD_DIM - COLS_S)))
    w_bc = w[..., COLS_A + COLS_S:].astype(BF16)
    return w_a, w_s, w_bc


def _encode(x, mem, tables_b, tables_c, p):
    dil_groups = tuple((H_B_GROUP, d) for _, d in DILATED_PAIRS)
    for l in range(p["w_out"].shape[0]):
        h = rmsnorm_rows(x, p["norm_mix_g"][l], BF16)
        proj_a = matmul(h, p["w_in_a"], l, tm=1024, tn=1024)
        proj_s = matmul(h, p["w_in_s"], l, tm=1024, tn=HEAD_DIM)
        proj_bc = matmul(h, p["w_in_bc"], l, tm=1024, tn=1024)
        qkv_a = deltanet_conv_prep(proj_a, p["conv_w"][l])
        gb = deltanet_gate_prep(proj_s, p["a_log"][l], p["dt_bias"][l])
        o_fwd, o_bwd = deltanet_recurrence(qkv_a, gb)
        q_b = head_prep(proj_bc, OFF_BQ, dil_groups, tables=tables_b, half=ROT_DIMS // 2, scale=SCALE)
        k_b = head_prep(proj_bc, OFF_BK, dil_groups, tables=tables_b, half=ROT_DIMS // 2)
        v_b = head_prep(proj_bc, OFF_BV, dil_groups)
        dil = [dilated_group_attention(q_b[gi], k_b[gi], v_b[gi], w, d) for gi, (w, d) in enumerate(DILATED_PAIRS)]
        q_c, = head_prep(proj_bc, OFF_CQ, ((H_C, 1),), gain=p["qnorm_g"][l], tables=tables_c, half=HEAD_DIM // 4,
                         scale=SCALE * LOG2_E)
        k_c, = head_prep(proj_bc, OFF_CK, ((KV_C, 1),), gain=p["knorm_g"][l], tables=tables_c, half=HEAD_DIM // 4)
        v_c, = head_prep(proj_bc, OFF_CV, ((KV_C, 1),), ones_pad=True)
        y_c = axial_flash_attention(q_c, k_c, v_c)
        mix = assemble_mix(o_fwd, o_bwd, proj_a, p["onorm_g"][l], [t[0] for t in dil], [t[1] for t in dil], y_c)
        x = matmul(mix, p["w_out"], l, tm=1024, tn=1024, residual=x)
        mem_n = rmsnorm_rows(mem, p["norm_mem_g"][l], BF16)
        kv = matmul(mem_n, p["wkv_x"], l, tm=N_MEM, tn=512, out_dtype=BF16)
        x, hf = cross_attention_block(x, p["norm_x_g"][l], p["wq_x"], kv, p["wo_x"], p["norm_ffn_g"][l], l)
        up = matmul(hf, p["w_up"], l, tm=1024, tn=1024, relu2=True, out_dtype=BF16)
        x = matmul(up, p["w_down"], l, tm=1024, tn=1024, tk=2048, residual=x)
    return rmsnorm_rows(x, p["norm_final_g"], F32)


def _tables(length):
    pos = jnp.arange(length, dtype=F32)
    tables_b = _rope_tables([pos], ROPE_THETA, ROT_DIMS)
    rows = length // GRID_W
    row_pos = jnp.repeat(jnp.arange(rows, dtype=F32), GRID_W)
    col_pos = jnp.tile(jnp.arange(GRID_W, dtype=F32), rows)
    tables_c = _rope_tables([row_pos, col_pos], AXIAL_THETA, HEAD_DIM // 2)
    return tables_b, tables_c


def kernel(x_prompt, x_sample, mem_prompt, mem_sample, norm_mix_g, w_in, conv_w, a_log, dt_bias, onorm_g, qnorm_g, knorm_g, w_out, norm_x_g, norm_mem_g, wq_x, wk_x, wv_x, wo_x, norm_ffn_g, w_up, w_down, norm_final_g):
    w_in_a, w_in_s, w_in_bc = _split_w_in(w_in)
    p = dict(
        norm_mix_g=norm_mix_g, conv_w=conv_w, a_log=a_log, dt_bias=dt_bias, onorm_g=onorm_g, qnorm_g=qnorm_g,
        knorm_g=knorm_g, norm_x_g=norm_x_g, norm_mem_g=norm_mem_g, norm_ffn_g=norm_ffn_g, norm_final_g=norm_final_g,
        w_in_a=w_in_a, w_in_s=w_in_s, w_in_bc=w_in_bc,
        w_out=w_out.astype(BF16), wq_x=wq_x.astype(BF16),
        wkv_x=jnp.concatenate([wk_x, wv_x], axis=-1).astype(BF16),
        wo_x=wo_x.astype(BF16), w_up=w_up.astype(BF16), w_down=w_down.astype(BF16),
    )
    outs = []
    for x, mem in ((x_prompt, mem_prompt), (x_sample, mem_sample)):
        length = x.shape[1]
        tables_b, tables_c = _tables(length)
        y = _encode(x[0], mem[0], tables_b, tables_c, p)
        outs.append(y[None])
    return tuple(outs)
```

```python
import functools
import math

import jax
import jax.numpy as jnp
from jax import lax
from jax.experimental import pallas as pl
from jax.experimental.pallas import tpu as pltpu

F32 = jnp.float32
BF16 = jnp.bfloat16

D_MODEL = 4096
HEAD_DIM = 128
H_A = 8
DILATED_PAIRS = ((128, 1), (512, 4), (2048, 16))
H_B_GROUP = 4
H_B = 12
H_C = 12
KV_C = 4
GQA_G = H_C // KV_C
CONV_W = 4
ROPE_THETA = 500000.0
ROT_DIMS = HEAD_DIM // 4
AXIAL_THETA = 10000.0
GRID_W = 64
N_MEM = 256
X_HEADS = 4
D_FF = 4 * D_MODEL
EPS = 1e-6
SCALE = HEAD_DIM ** -0.5
LOG2_E = math.log2(math.e)
NEG_BIG = -1e30

W_A = H_A * HEAD_DIM
W_B = H_B * HEAD_DIM
W_CQ = H_C * HEAD_DIM
W_CKV = KV_C * HEAD_DIM
COLS_A = 4 * W_A
COLS_S = 4 * H_A
COLS_BC = 3 * W_B + W_CQ + 2 * W_CKV
OFF_AGATE = 3 * W_A
OFF_BQ = 0
OFF_BK = OFF_BQ + W_B
OFF_BV = OFF_BK + W_B
OFF_CQ = OFF_BV + W_B
OFF_CK = OFF_CQ + W_CQ
OFF_CV = OFF_CK + W_CKV

V7X_VMEM_LIMIT_BYTES = 56 * 1024 * 1024
DN_CHUNK = 64


def _params(*sem):
    return pltpu.CompilerParams(dimension_semantics=sem, vmem_limit_bytes=V7X_VMEM_LIMIT_BYTES)


def _rmsnorm_kernel(x_ref, g_ref, o_ref):
    x = x_ref[...]
    ms = jnp.mean(x * x, axis=-1, keepdims=True)
    o_ref[...] = (x * lax.rsqrt(ms + EPS) * g_ref[...]).astype(o_ref.dtype)


def rmsnorm_rows(x, g, out_dtype, tm=256):
    m, d = x.shape
    return pl.pallas_call(
        _rmsnorm_kernel,
        grid=(m // tm,),
        in_specs=[pl.BlockSpec((tm, d), lambda i: (i, 0)), pl.BlockSpec((1, d), lambda i: (0, 0))],
        out_specs=pl.BlockSpec((tm, d), lambda i: (i, 0)),
        out_shape=jax.ShapeDtypeStruct((m, d), out_dtype),
        name="rmsnorm_rows",
        compiler_params=_params("parallel"),
    )(x, g.reshape(1, d))


def _mm_kernel(*refs, nk, relu2, has_res, use_scratch):
    a_ref, w_ref = refs[0], refs[1]
    r_ref = refs[2] if has_res else None
    o_ref = refs[3] if has_res else refs[2]
    def part():
        return jnp.dot(a_ref[...], w_ref[...], preferred_element_type=F32)

    def finish(acc):
        if relu2:
            acc = jnp.square(jnp.maximum(acc, 0.0))
        return acc.astype(o_ref.dtype)

    if nk == 1:
        o_ref[...] = finish(r_ref[...] + part() if has_res else part())
        return
    acc_ref = refs[-1] if use_scratch else o_ref
    k = pl.program_id(2)

    @pl.when(k == 0)
    def _():
        acc_ref[...] = (r_ref[...] + part()) if has_res else part()

    @pl.when(k > 0)
    def _():
        acc_ref[...] = acc_ref[...] + part()

    if use_scratch or relu2:
        @pl.when(k == nk - 1)
        def _():
            o_ref[...] = finish(acc_ref[...])


def matmul(a, w, layer, *, tm, tn, tk=None, residual=None, relu2=False, out_dtype=F32):
    m, kd = a.shape
    n = w.shape[2]
    tm = min(tm, m)
    tk = kd if tk is None else tk
    nk = kd // tk
    has_res = residual is not None
    use_scratch = nk > 1 and out_dtype != F32
    in_specs = [pl.BlockSpec((tm, tk), lambda i, j, k: (i, k)),
                pl.BlockSpec((None, tk, tn), lambda i, j, k: (layer, k, j))]
    args = [a, w]
    if has_res:
        in_specs.append(pl.BlockSpec((tm, tn), lambda i, j, k: (i, j)))
        args.append(residual)
    return pl.pallas_call(
        functools.partial(_mm_kernel, nk=nk, relu2=relu2, has_res=has_res, use_scratch=use_scratch),
        grid=(m // tm, n // tn, nk),
        in_specs=in_specs,
        out_specs=pl.BlockSpec((tm, tn), lambda i, j, k: (i, j)),
        out_shape=jax.ShapeDtypeStruct((m, n), out_dtype),
        scratch_shapes=[pltpu.VMEM((tm, tn), F32)] if use_scratch else [],
        name="matmul",
        compiler_params=_params("parallel", "parallel", "arbitrary"),
    )(*args)


def _rope_tables(pos_list, theta, width_each):
    half = width_each // 2
    inv = theta ** (-jnp.arange(half, dtype=F32) / half)
    cos_parts, sin_parts = [], []
    for pos in pos_list:
        ang = pos[:, None] * inv[None, :]
        c, s = jnp.cos(ang), jnp.sin(ang)
        cos_parts += [c, c]
        sin_parts += [-s, s]
    length = pos_list[0].shape[0]
    rest = HEAD_DIM - width_each * len(pos_list)
    if rest:
        cos_parts.append(jnp.ones((length, rest), F32))
        sin_parts.append(jnp.zeros((length, rest), F32))
    return jnp.concatenate(cos_parts, axis=1), jnp.concatenate(sin_parts, axis=1)


def _head_prep_kernel(*refs, groups, tm, half, use_norm, use_rope, scale, ones_pad):
    x_ref = refs[0]
    n_out = len(groups)
    o_refs = refs[-n_out - 1:-1]
    buf_ref = refs[-1]
    idx = 1
    if use_norm:
        g = refs[idx][...]
        idx += 1
    if use_rope:
        cos = refs[idx][...]
        sin = refs[idx + 1][...]
        src = lax.broadcasted_iota(jnp.int32, (2 * HEAD_DIM, HEAD_DIM), 0) % HEAD_DIM
        dst = lax.broadcasted_iota(jnp.int32, (2 * HEAD_DIM, HEAD_DIM), 1)
        partner_of_dst = jnp.where((dst % (2 * half)) < half, dst + half, dst - half)
        swap = (src == partner_of_dst).astype(BF16)
    h = 0
    for o_ref, (n_heads, dilation) in zip(o_refs, groups):
        for hg in range(n_heads):
            x = x_ref[:, h * HEAD_DIM:(h + 1) * HEAD_DIM]
            if use_norm:
                ms = jnp.mean(x * x, axis=-1, keepdims=True)
                x = x * lax.rsqrt(ms + EPS) * g
            if use_rope:
                hi = x.astype(BF16)
                lo = (x - hi.astype(F32)).astype(BF16)
                partner = jnp.dot(jnp.concatenate([hi, lo], axis=1), swap, preferred_element_type=F32)
                x = x * cos + partner * sin
            if scale != 1.0:
                x = x * scale
            if ones_pad:
                o_ref[hg, :, :HEAD_DIM] = x.astype(o_ref.dtype)
                o_ref[hg, :, HEAD_DIM:] = jnp.ones((tm, HEAD_DIM), o_ref.dtype)
            elif dilation == 1:
                o_ref[hg] = x.astype(o_ref.dtype)
            else:
                buf_ref[h] = x
                for r in range(dilation):
                    o_ref[hg, :, r * HEAD_DIM:(r + 1) * HEAD_DIM] = buf_ref[
                        h, pl.ds(r, tm // dilation, stride=dilation), :].astype(o_ref.dtype)
            h += 1


def head_prep(proj, col_off, groups, *, gain=None, tables=None, half=None, scale=1.0, ones_pad=False, tm=512):
    length = proj.shape[0]
    n_heads = sum(n for n, _ in groups)
    width = n_heads * HEAD_DIM
    in_specs = [pl.BlockSpec((tm, width), lambda i: (i, col_off // width))]
    args = [proj]
    if gain is not None:
        in_specs.append(pl.BlockSpec((1, HEAD_DIM), lambda i: (0, 0)))
        args.append(gain.reshape(1, HEAD_DIM))
    if tables is not None:
        in_specs += [pl.BlockSpec((tm, HEAD_DIM), lambda i: (i, 0))] * 2
        args += list(tables)
    lanes = 2 if ones_pad else 1
    outs = pl.pallas_call(
        functools.partial(_head_prep_kernel, groups=groups, tm=tm, half=half, use_norm=gain is not None,
                          use_rope=tables is not None, scale=scale, ones_pad=ones_pad),
        grid=(length // tm,),
        in_specs=in_specs,
        out_specs=[pl.BlockSpec((n, tm // d, lanes * d * HEAD_DIM), lambda i: (0, i, 0)) for n, d in groups],
        out_shape=[jax.ShapeDtypeStruct((n, length // d, lanes * d * HEAD_DIM), BF16) for n, d in groups],
        scratch_shapes=[pltpu.VMEM((n_heads, tm, HEAD_DIM), F32)],
        name="head_prep",
        compiler_params=_params("parallel"),
    )(*args)
    return outs


def _flash_kernel(q_ref, k_ref, v_ref, o_ref, m_ref, acc_ref, *, tq, nkv, chunk, nchunk):
    ki = pl.program_id(2)

    @pl.when(ki == 0)
    def _():
        m_ref[...] = jnp.full(m_ref.shape, NEG_BIG, F32)
        acc_ref[...] = jnp.zeros(acc_ref.shape, F32)

    q = q_ref[...].reshape(GQA_G * tq, HEAD_DIM)
    ntile = chunk // HEAD_DIM

    def scores(j):
        return lax.dot_general(q, k_ref[0, j * chunk:(j + 1) * chunk, :], (((1,), (1,)), ((), ())),
                               preferred_element_type=F32)

    heads = range(GQA_G)
    rows = [slice(g * tq, (g + 1) * tq) for g in heads]
    m = [m_ref[r, :] for r in rows]
    acc = [acc_ref[r, :] for r in rows]
    s_next = scores(0)
    for j in range(nchunk):
        s = s_next
        if j + 1 < nchunk:
            s_next = scores(j + 1)
        alphas, p_rows = [], []
        for g in heads:
            tiles = [s[rows[g], t * HEAD_DIM:(t + 1) * HEAD_DIM] for t in range(ntile)]
            part = tiles[0]
            for t in tiles[1:]:
                part = jnp.maximum(part, t)
            m_new = jnp.maximum(m[g], jnp.max(part, axis=-1, keepdims=True))
            alpha = jnp.exp2(m[g] - m_new)
            alphas.append(jnp.concatenate([alpha, alpha], axis=1))
            p_rows.append(jnp.concatenate([jnp.exp2(t - m_new).astype(BF16) for t in tiles], axis=1))
            m[g] = m_new
        pv = jnp.dot(jnp.concatenate(p_rows, axis=0), v_ref[0, j * chunk:(j + 1) * chunk, :],
                     preferred_element_type=F32)
        for g in heads:
            acc[g] = alphas[g] * acc[g] + pv[rows[g]]
    for g in heads:
        m_ref[rows[g], :], acc_ref[rows[g], :] = m[g], acc[g]

    @pl.when(ki == nkv - 1)
    def _():
        for g in heads:
            o = acc[g][:, :HEAD_DIM] / acc[g][:, HEAD_DIM:]
            o_ref[:, g * HEAD_DIM:(g + 1) * HEAD_DIM] = o.astype(o_ref.dtype)


def axial_flash_attention(q, k, v, *, tq=256, tk=16384, chunk=512):
    length = q.shape[1]
    tk = min(tk, length)
    chunk = min(chunk, tk)
    nkv = length // tk
    rows = GQA_G * tq
    return pl.pallas_call(
        functools.partial(_flash_kernel, tq=tq, nkv=nkv, chunk=chunk, nchunk=tk // chunk),
        grid=(KV_C, length // tq, nkv),
        in_specs=[pl.BlockSpec((GQA_G, tq, HEAD_DIM), lambda h, i, j: (h, i, 0)),
                  pl.BlockSpec((1, tk, HEAD_DIM), lambda h, i, j: (h, j, 0)),
                  pl.BlockSpec((1, tk, 2 * HEAD_DIM), lambda h, i, j: (h, j, 0))],
        out_specs=pl.BlockSpec((tq, GQA_G * HEAD_DIM), lambda h, i, j: (i, h)),
        out_shape=jax.ShapeDtypeStruct((length, H_C * HEAD_DIM), BF16),
        scratch_shapes=[pltpu.VMEM((rows, HEAD_DIM), F32), pltpu.VMEM((rows, 2 * HEAD_DIM), F32)],
        name="axial_flash",
        compiler_params=_params("parallel", "parallel", "arbitrary"),
    )(q, k, v)


def _dilated_kernel(q_ref, kp_ref, kc_ref, kn_ref, vp_ref, vc_ref, vn_ref, o_ref, lse_ref, *, tq, side, n_sub):
    c = pl.program_id(1)
    row = lax.broadcasted_iota(jnp.int32, (tq, tq + 2 * side), 0)
    col = lax.broadcasted_iota(jnp.int32, (tq, tq + 2 * side), 1)
    key_idx = c * tq - side + col
    valid = (jnp.abs(row + side - col) <= side) & (key_idx >= 0) & (key_idx < n_sub)
    heads = range(H_B_GROUP)
    halo = lambda p_ref, c_ref, n_ref, h: jnp.concatenate([p_ref[h, tq - side:, :], c_ref[h], n_ref[h, :side, :]], axis=0)
    ss = [lax.dot_general(q_ref[h], halo(kp_ref, kc_ref, kn_ref, h), (((1,), (1,)), ((), ())),
                          preferred_element_type=F32) for h in heads]
    ss = [jnp.where(valid, s, NEG_BIG) for s in ss]
    ms = [jnp.max(s, axis=-1, keepdims=True) for s in ss]
    es = [jnp.exp(s - m) for s, m in zip(ss, ms)]
    dens = [jnp.sum(e, axis=-1, keepdims=True) for e in es]
    os_ = [jnp.dot(e.astype(BF16), halo(vp_ref, vc_ref, vn_ref, h), preferred_element_type=F32)
           for e, h in zip(es, heads)]
    for h in heads:
        o_ref[h] = os_[h] / dens[h]
        lse_ref[h] = jnp.broadcast_to(ms[h] + jnp.log(dens[h]), (tq, HEAD_DIM))


def dilated_group_attention(q, k, v, window, dilation, *, tq=256):
    d = dilation
    n_sub = q.shape[1]
    side = (window // 2) // d
    nblk = n_sub // tq
    cur = lambda r, c: (0, c, r)
    prev = lambda r, c: (0, jnp.maximum(c - 1, 0), r)
    nxt = lambda r, c: (0, jnp.minimum(c + 1, nblk - 1), r)
    blk = (H_B_GROUP, tq, HEAD_DIM)
    return pl.pallas_call(
        functools.partial(_dilated_kernel, tq=tq, side=side, n_sub=n_sub),
        grid=(d, nblk),
        in_specs=[pl.BlockSpec(blk, cur), pl.BlockSpec(blk, prev), pl.BlockSpec(blk, cur), pl.BlockSpec(blk, nxt),
                  pl.BlockSpec(blk, prev), pl.BlockSpec(blk, cur), pl.BlockSpec(blk, nxt)],
        out_specs=[pl.BlockSpec(blk, cur)] * 2,
        out_shape=[jax.ShapeDtypeStruct((H_B_GROUP, n_sub, d * HEAD_DIM), F32)] * 2,
        name="dilated_attn",
        compiler_params=_params("parallel", "parallel"),
    )(q, k, k, k, v, v, v)


def _conv_prep_kernel(cur_ref, prev_ref, next_ref, w_ref, o_ref, *, tm, nblk):
    i = pl.program_id(0)
    j = pl.program_id(1)
    x = cur_ref[...]
    before = jnp.where(i > 0, prev_ref[7:8, :], 0.0)
    after = jnp.where(i < nblk - 1, next_ref[0:2, :], 0.0)
    row = lax.broadcasted_iota(jnp.int32, x.shape, 0)
    xm1 = jnp.where(row == 0, before, pltpu.roll(x, 1, axis=0))
    xp1 = jnp.where(row == tm - 1, after[0:1, :], pltpu.roll(x, tm - 1, axis=0))
    xp2 = jnp.where(row == tm - 2, after[0:1, :],
                    jnp.where(row == tm - 1, after[1:2, :], pltpu.roll(x, tm - 2, axis=0)))
    w = w_ref[...]
    y = xm1 * w[0:1, :] + x * w[1:2, :] + xp1 * w[2:3, :] + xp2 * w[3:4, :]
    y = y * jax.nn.sigmoid(y)
    post = jnp.where(j == 0, SCALE, 1.0)
    for h in range(H_A):
        yh = y[:, h * HEAD_DIM:(h + 1) * HEAD_DIM]
        inv = lax.rsqrt(jnp.sum(yh * yh, axis=-1, keepdims=True) + EPS) * post
        o_ref[:, h * HEAD_DIM:(h + 1) * HEAD_DIM] = yh * jnp.where(j < 2, inv, 1.0)


def deltanet_conv_prep(proj, conv_w, *, tm=256):
    length = proj.shape[0]
    nblk = length // tm
    cb = 0
    hb = tm // 8
    return pl.pallas_call(
        functools.partial(_conv_prep_kernel, tm=tm, nblk=nblk),
        grid=(nblk, 3),
        in_specs=[pl.BlockSpec((tm, W_A), lambda i, j: (i, cb + j)),
                  pl.BlockSpec((8, W_A), lambda i, j: (jnp.maximum(i * hb - 1, 0), cb + j)),
                  pl.BlockSpec((8, W_A), lambda i, j: (jnp.minimum((i + 1) * hb, nblk * hb - 1), cb + j)),
                  pl.BlockSpec((CONV_W, W_A), lambda i, j: (0, j))],
        out_specs=pl.BlockSpec((tm, W_A), lambda i, j: (i, j)),
        out_shape=jax.ShapeDtypeStruct((length, 3 * W_A), F32),
        name="delta_conv_prep",
        compiler_params=_params("parallel", "parallel"),
    )(proj, proj, proj, conv_w)


def _gate_prep_kernel(x_ref, a_ref, dt_ref, o_ref):
    x = x_ref[...]
    lane = lax.broadcasted_iota(jnp.int32, x.shape, 1)
    beta = jax.nn.sigmoid(x)
    z = x + dt_ref[...]
    softplus = jnp.maximum(z, 0.0) + jnp.log(1.0 + jnp.exp(-jnp.abs(z)))
    g = -jnp.exp(a_ref[...]) * softplus
    o_ref[...] = jnp.where(lane < 2 * H_A, beta, jnp.where(lane < 4 * H_A, g, 0.0))


def deltanet_gate_prep(proj, a_log, dt_bias, *, tm=512):
    length = proj.shape[0]
    pad = lambda t: jnp.zeros((1, HEAD_DIM), F32).at[0, 2 * H_A:4 * H_A].set(t.reshape(-1))
    return pl.pallas_call(
        _gate_prep_kernel,
        grid=(length // tm,),
        in_specs=[pl.BlockSpec((tm, HEAD_DIM), lambda i: (i, 0)),
                  pl.BlockSpec((1, HEAD_DIM), lambda i: (0, 0)),
                  pl.BlockSpec((1, HEAD_DIM), lambda i: (0, 0))],
        out_specs=pl.BlockSpec((tm, HEAD_DIM), lambda i: (i, 0)),
        out_shape=jax.ShapeDtypeStruct((length, HEAD_DIM), F32),
        name="delta_gate_prep",
        compiler_params=_params("parallel"),
    )(proj, pad(a_log), pad(dt_bias))


def _bdot(a, b):
    return jnp.dot(a.astype(BF16), b.astype(BF16), preferred_element_type=F32)


def _split_dot(tri, x):
    hi = x.astype(BF16)
    lo = (x - hi.astype(F32)).astype(BF16)
    return jnp.dot(tri, hi, preferred_element_type=F32) + jnp.dot(tri, lo, preferred_element_type=F32)


def _delta_kernel(qf, kf, vf, gf, qb, kb, vb, gbk, of_ref, ob_ref, s_ref, *, chunk):
    @pl.when(pl.program_id(0) == 0)
    def _():
        s_ref[...] = jnp.zeros(s_ref.shape, F32)

    cc = chunk
    row = lax.broadcasted_iota(jnp.int32, (cc, cc), 0)
    col = lax.broadcasted_iota(jnp.int32, (cc, cc), 1)
    eye = (row == col).astype(F32)
    level_masks = []
    differ = row ^ col
    b = 2
    while b <= cc:
        level_masks.append((differ < b) & (differ >= b // 2))
        b *= 2

    chains = []
    for direction, (q_ref, k_ref, v_ref, gb_ref, o_ref) in enumerate(
            ((qf, kf, vf, gf, of_ref), (qb, kb, vb, gbk, ob_ref))):
        incl = (row >= col) if direction == 0 else (row <= col)
        strict = (row > col) if direction == 0 else (row < col)
        last = cc - 1 if direction == 0 else 0
        gb = gb_ref[...]
        gc_all = _split_dot(incl.astype(BF16), gb)
        gc_rows = jnp.concatenate([gc_all, jnp.zeros_like(gc_all)], axis=0).T if cc < HEAD_DIM else gc_all.T
        for h in range(H_A):
            lane_beta = direction * H_A + h
            lane_g = 2 * H_A + direction * H_A + h
            sl = slice(h * HEAD_DIM, (h + 1) * HEAD_DIM)
            chains.append(dict(
                q=q_ref[:, sl], k=k_ref[:, sl], v=v_ref[:, sl], o_ref=o_ref, sl=sl, idx=direction * H_A + h,
                beta=gb[:, lane_beta:lane_beta + 1],
                gcc=gc_all[:, lane_g:lane_g + 1],
                gcr=gc_rows[lane_g:lane_g + 1, 0:cc],
                incl=incl, strict=strict, last=last))

    for c in chains:
        c["decay"] = jnp.exp(jnp.where(c["incl"], c["gcc"] - c["gcr"], NEG_BIG))
        c["egc"] = jnp.exp(c["gcc"])
        c["kb"] = c["k"] * c["beta"]
        c["vb"] = c["v"] * c["beta"]
    nt = (((1,), (1,)), ((), ()))
    kks = [lax.dot_general(jnp.concatenate([c["kb"], c["q"]], axis=0).astype(BF16), c["k"].astype(BF16), nt,
                           preferred_element_type=F32) for c in chains]
    for c, kk in zip(chains, kks):
        c["a"] = jnp.where(c["strict"], kk[:cc] * c["decay"], 0.0)
        c["qk"] = kk[cc:] * c["decay"]
    ts = [eye - jnp.where(level_masks[0], c["a"], 0.0) for c in chains]
    for mask in level_masks[1:]:
        tb = [_bdot(t, jnp.where(mask, c["a"], 0.0)) for t, c in zip(ts, chains)]
        tbt = [_bdot(x, t) for x, t in zip(tb, ts)]
        ts = [t - x for t, x in zip(ts, tbt)]
    uws = [_bdot(t, jnp.concatenate([c["vb"], c["kb"] * c["egc"]], axis=1)) for t, c in zip(ts, chains)]
    s_olds = [s_ref[c["idx"]] for c in chains]
    wqs = [_bdot(jnp.concatenate([uw[:, HEAD_DIM:], c["q"] * c["egc"]], axis=0), s_old)
           for uw, c, s_old in zip(uws, chains, s_olds)]
    v_news = [uw[:, :HEAD_DIM] - wq[:cc] for uw, wq in zip(uws, wqs)]
    intra = [_bdot(c["qk"], v_new) for c, v_new in zip(chains, v_news)]
    for c, wq, x in zip(chains, wqs, intra):
        c["o_ref"][:, c["sl"]] = wq[cc:] + x
    tn = (((0,), (0,)), ((), ()))
    g_lasts = [c["gcc"][c["last"]:c["last"] + 1, :] for c in chains]
    kvs = [lax.dot_general((c["k"] * jnp.exp(g_last - c["gcc"])).astype(BF16), v_new.astype(BF16), tn,
                           preferred_element_type=F32) for c, g_last, v_new in zip(chains, g_lasts, v_news)]
    for c, s_old, g_last, kv in zip(chains, s_olds, g_lasts, kvs):
        s_ref[c["idx"]] = s_old * jnp.exp(g_last) + kv


def deltanet_recurrence(qkv, gb, *, chunk=DN_CHUNK):
    length = qkv.shape[0]
    n = length // chunk
    fwd = lambda j: (lambda c: (c, j))
    bwd = lambda j: (lambda c: (n - 1 - c, j))
    col = lambda m, width: pl.BlockSpec((chunk, width), m)
    return pl.pallas_call(
        functools.partial(_delta_kernel, chunk=chunk),
        grid=(n,),
        in_specs=[col(fwd(0), W_A), col(fwd(1), W_A), col(fwd(2), W_A), col(fwd(0), HEAD_DIM),
                  col(bwd(0), W_A), col(bwd(1), W_A), col(bwd(2), W_A), col(bwd(0), HEAD_DIM)],
        out_specs=[col(fwd(0), W_A), col(bwd(0), W_A)],
        out_shape=[jax.ShapeDtypeStruct((length, W_A), F32)] * 2,
        scratch_shapes=[pltpu.VMEM((2 * H_A, HEAD_DIM, HEAD_DIM), F32)],
        name="delta_recurrence",
        compiler_params=_params("arbitrary"),
    )(qkv, qkv, qkv, gb, qkv, qkv, qkv, gb)


def _assemble_kernel(of_ref, ob_ref, gate_ref, og_ref, o0, o1, o2, l0, l1, l2, yc_ref, mix_ref, buf_ref, *, tm):
    og = og_ref[...]
    for h in range(H_A):
        sl = slice(h * HEAD_DIM, (h + 1) * HEAD_DIM)
        o = of_ref[:, sl] + ob_ref[:, sl]
        ms = jnp.mean(o * o, axis=-1, keepdims=True)
        gate = gate_ref[:, sl]
        mix_ref[:, sl] = (o * lax.rsqrt(ms + EPS) * og * (gate * jax.nn.sigmoid(gate))).astype(mix_ref.dtype)

    def positions(ref, h, gi, slot):
        d = DILATED_PAIRS[gi][1]
        if d == 1:
            return ref[h]
        for r in range(d):
            buf_ref[slot, pl.ds(r, tm // d, stride=d), :] = ref[h, :, r * HEAD_DIM:(r + 1) * HEAD_DIM]
        return buf_ref[slot]

    outs, lses = (o0, o1, o2), (l0, l1, l2)
    n_groups = len(DILATED_PAIRS)
    for h in range(H_B_GROUP):
        ls = [positions(lses[gi], h, gi, (h * n_groups + gi) * 2) for gi in range(n_groups)]
        mx = jnp.maximum(jnp.maximum(ls[0], ls[1]), ls[2])
        es = [jnp.exp(t - mx) for t in ls]
        tot = es[0] + es[1] + es[2]
        for gi in range(n_groups):
            c0 = W_A + (gi * H_B_GROUP + h) * HEAD_DIM
            o = positions(outs[gi], h, gi, (h * n_groups + gi) * 2 + 1)
            mix_ref[:, c0:c0 + HEAD_DIM] = (o * (es[gi] / tot)).astype(mix_ref.dtype)
    mix_ref[:, W_A + W_B:] = yc_ref[...]


def assemble_mix(o_fwd, o_bwd, proj, onorm_g, dil_outs, dil_lses, y_c, *, tm=256):
    length = o_fwd.shape[0]
    row = lambda w: pl.BlockSpec((tm, w), lambda i: (i, 0))
    hblks = [pl.BlockSpec((H_B_GROUP, tm // d, d * HEAD_DIM), lambda i: (0, i, 0)) for _, d in DILATED_PAIRS]
    return pl.pallas_call(
        functools.partial(_assemble_kernel, tm=tm),
        grid=(length // tm,),
        in_specs=[row(W_A), row(W_A), pl.BlockSpec((tm, W_A), lambda i: (i, OFF_AGATE // W_A)),
                  pl.BlockSpec((1, HEAD_DIM), lambda i: (0, 0))] + hblks * 2 + [row(W_CQ)],
        out_specs=row(D_MODEL),
        out_shape=jax.ShapeDtypeStruct((length, D_MODEL), BF16),
        scratch_shapes=[pltpu.VMEM((2 * H_B, tm, HEAD_DIM), F32)],
        name="assemble_mix",
        compiler_params=_params("parallel"),
    )(o_fwd, o_bwd, proj, onorm_g.reshape(1, HEAD_DIM), *dil_outs, *dil_lses, y_c)


def _cross_kernel(x_ref, gx_ref, wq_ref, k_ref, v_ref, wo_ref, gf_ref, xo_ref, hf_ref):
    x = x_ref[...]
    ms = jnp.mean(x * x, axis=-1, keepdims=True)
    h = (x * lax.rsqrt(ms + EPS) * gx_ref[...]).astype(BF16)
    q = jnp.dot(h, wq_ref[...], preferred_element_type=F32)
    sls = [slice(hh * HEAD_DIM, (hh + 1) * HEAD_DIM) for hh in range(X_HEADS)]
    ss = [lax.dot_general(q[:, sl].astype(BF16), k_ref[:, sl], (((1,), (1,)), ((), ())),
                          preferred_element_type=F32) * SCALE for sl in sls]
    es = [jnp.exp(s - jnp.max(s, axis=-1, keepdims=True)) for s in ss]
    ps = [e / jnp.sum(e, axis=-1, keepdims=True) for e in es]
    heads = [jnp.dot(p.astype(BF16), v_ref[:, sl], preferred_element_type=F32).astype(BF16) for p, sl in zip(ps, sls)]
    o = jnp.concatenate(heads, axis=1)
    xn = x + jnp.dot(o, wo_ref[...], preferred_element_type=F32)
    xo_ref[...] = xn
    ms2 = jnp.mean(xn * xn, axis=-1, keepdims=True)
    hf_ref[...] = (xn * lax.rsqrt(ms2 + EPS) * gf_ref[...]).astype(hf_ref.dtype)


def cross_attention_block(x, g_x, wq, kv, wo, g_ffn, layer, *, tm=256):
    length = x.shape[0]
    width = X_HEADS * HEAD_DIM
    full = lambda shape: pl.BlockSpec(shape, lambda i: (0, 0))
    stacked = lambda shape: pl.BlockSpec((None,) + shape, lambda i: (layer, 0, 0))
    row = pl.BlockSpec((tm, D_MODEL), lambda i: (i, 0))
    return pl.pallas_call(
        _cross_kernel,
        grid=(length // tm,),
        in_specs=[row, full((1, D_MODEL)), stacked((D_MODEL, width)),
                  pl.BlockSpec((N_MEM, width), lambda i: (0, 0)), pl.BlockSpec((N_MEM, width), lambda i: (0, 1)),
                  stacked((width, D_MODEL)), full((1, D_MODEL))],
        out_specs=[row, row],
        out_shape=[jax.ShapeDtypeStruct((length, D_MODEL), F32), jax.ShapeDtypeStruct((length, D_MODEL), BF16)],
        name="cross_attn",
        compiler_params=_params("parallel"),
    )(x, g_x.reshape(1, D_MODEL), wq, kv, kv, wo, g_ffn.reshape(1, D_MODEL))


def _split_w_in(w):
    w_a = w[..., :COLS_A].astype(BF16)
    w_s = jnp.pad(w[..., COLS_A:COLS_A + COLS_S].astype(BF16), ((0, 0), (0, 0), (0, HEAD_DIM - COLS_S)))
    w_bc = w[..., COLS_A + COLS_S:].astype(BF16)
    return w_a, w_s, w_bc


def _encode(x, mem, tables_b, tables_c, p):
    dil_groups = tuple((H_B_GROUP, d) for _, d in DILATED_PAIRS)
    for l in range(p["w_out"].shape[0]):
        h = rmsnorm_rows(x, p["norm_mix_g"][l], BF16)
        proj_a = matmul(h, p["w_in_a"], l, tm=1024, tn=1024)
        proj_s = matmul(h, p["w_in_s"], l, tm=1024, tn=HEAD_DIM)
        proj_bc = matmul(h, p["w_in_bc"], l, tm=1024, tn=1024)
        qkv_a = deltanet_conv_prep(proj_a, p["conv_w"][l])
        gb = deltanet_gate_prep(proj_s, p["a_log"][l], p["dt_bias"][l])
        o_fwd, o_bwd = deltanet_recurrence(qkv_a, gb)
        q_b = head_prep(proj_bc, OFF_BQ, dil_groups, tables=tables_b, half=ROT_DIMS // 2, scale=SCALE)
        k_b = head_prep(proj_bc, OFF_BK, dil_groups, tables=tables_b, half=ROT_DIMS // 2)
        v_b = head_prep(proj_bc, OFF_BV, dil_groups)
        dil = [dilated_group_attention(q_b[gi], k_b[gi], v_b[gi], w, d) for gi, (w, d) in enumerate(DILATED_PAIRS)]
        q_c, = head_prep(proj_bc, OFF_CQ, ((H_C, 1),), gain=p["qnorm_g"][l], tables=tables_c, half=HEAD_DIM // 4,
                         scale=SCALE * LOG2_E)
        k_c, = head_prep(proj_bc, OFF_CK, ((KV_C, 1),), gain=p["knorm_g"][l], tables=tables_c, half=HEAD_DIM // 4)
        v_c, = head_prep(proj_bc, OFF_CV, ((KV_C, 1),), ones_pad=True)
        y_c = axial_flash_attention(q_c, k_c, v_c)
        mix = assemble_mix(o_fwd, o_bwd, proj_a, p["onorm_g"][l], [t[0] for t in dil], [t[1] for t in dil], y_c)
        x = matmul(mix, p["w_out"], l, tm=1024, tn=1024, residual=x)
        mem_n = rmsnorm_rows(mem, p["norm_mem_g"][l], BF16)
        kv = matmul(mem_n, p["wkv_x"], l, tm=N_MEM, tn=512, out_dtype=BF16)
        x, hf = cross_attention_block(x, p["norm_x_g"][l], p["wq_x"], kv, p["wo_x"], p["norm_ffn_g"][l], l)
        up = matmul(hf, p["w_up"], l, tm=1024, tn=1024, relu2=True, out_dtype=BF16)
        x = matmul(up, p["w_down"], l, tm=1024, tn=1024, tk=2048, residual=x)
    return rmsnorm_rows(x, p["norm_final_g"], F32)


def _tables(length):
    pos = jnp.arange(length, dtype=F32)
    tables_b = _rope_tables([pos], ROPE_THETA, ROT_DIMS)
    rows = length // GRID_W
    row_pos = jnp.repeat(jnp.arange(rows, dtype=F32), GRID_W)
    col_pos = jnp.tile(jnp.arange(GRID_W, dtype=F32), rows)
    tables_c = _rope_tables([row_pos, col_pos], AXIAL_THETA, HEAD_DIM // 2)
    return tables_b, tables_c


def kernel(x_prompt, x_sample, mem_prompt, mem_sample, norm_mix_g, w_in, conv_w, a_log, dt_bias, onorm_g, qnorm_g, knorm_g, w_out, norm_x_g, norm_mem_g, wq_x, wk_x, wv_x, wo_x, norm_ffn_g, w_up, w_down, norm_final_g):
    w_in_a, w_in_s, w_in_bc = _split_w_in(w_in)
    p = dict(
        norm_mix_g=norm_mix_g, conv_w=conv_w, a_log=a_log, dt_bias=dt_bias, onorm_g=onorm_g, qnorm_g=qnorm_g,
        knorm_g=knorm_g, norm_x_g=norm_x_g, norm_mem_g=norm_mem_g, norm_ffn_g=norm_ffn_g, norm_final_g=norm_final_g,
        w_in_a=w_in_a, w_in_s=w_in_s, w_in_bc=w_in_bc,
        w_out=w_out.astype(BF16), wq_x=wq_x.astype(BF16),
        wkv_x=jnp.concatenate([wk_x, wv_x], axis=-1).astype(BF16),
        wo_x=wo_x.astype(BF16), w_up=w_up.astype(BF16), w_down=w_down.astype(BF16),
    )
    outs = []
    for x, mem in ((x_prompt, mem_prompt), (x_sample, mem_sample)):
        length = x.shape[1]
        tables_b, tables_c = _tables(length)
        y = _encode(x[0], mem[0], tables_b, tables_c, p)
        outs.append(y[None])
    return tuple(outs)
```

```python
import functools
import math

import jax
import jax.numpy as jnp
from jax import lax
from jax.experimental import pallas as pl
from jax.experimental.pallas import tpu as pltpu

F32 = jnp.float32
BF16 = jnp.bfloat16

D_MODEL = 4096
HEAD_DIM = 128
H_A = 8
DILATED_PAIRS = ((128, 1), (512, 4), (2048, 16))
H_B_GROUP = 4
H_B = 12
H_C = 12
KV_C = 4
GQA_G = H_C // KV_C
CONV_W = 4
ROPE_THETA = 500000.0
ROT_DIMS = HEAD_DIM // 4
AXIAL_THETA = 10000.0
GRID_W = 64
N_MEM = 256
X_HEADS = 4
D_FF = 4 * D_MODEL
EPS = 1e-6
SCALE = HEAD_DIM ** -0.5
LOG2_E = math.log2(math.e)
NEG_BIG = -1e30

W_A = H_A * HEAD_DIM
W_B = H_B * HEAD_DIM
W_CQ = H_C * HEAD_DIM
W_CKV = KV_C * HEAD_DIM
COLS_A = 4 * W_A
COLS_S = 4 * H_A
COLS_BC = 3 * W_B + W_CQ + 2 * W_CKV
OFF_AGATE = 3 * W_A
OFF_BQ = 0
OFF_BK = OFF_BQ + W_B
OFF_BV = OFF_BK + W_B
OFF_CQ = OFF_BV + W_B
OFF_CK = OFF_CQ + W_CQ
OFF_CV = OFF_CK + W_CKV

V7X_VMEM_LIMIT_BYTES = 56 * 1024 * 1024
DN_CHUNK = 64


def _params(*sem):
    return pltpu.CompilerParams(dimension_semantics=sem, vmem_limit_bytes=V7X_VMEM_LIMIT_BYTES)


def _rmsnorm_kernel(x_ref, g_ref, o_ref):
    x = x_ref[...]
    ms = jnp.mean(x * x, axis=-1, keepdims=True)
    o_ref[...] = (x * lax.rsqrt(ms + EPS) * g_ref[...]).astype(o_ref.dtype)


def rmsnorm_rows(x, g, out_dtype, tm=256):
    m, d = x.shape
    return pl.pallas_call(
        _rmsnorm_kernel,
        grid=(m // tm,),
        in_specs=[pl.BlockSpec((tm, d), lambda i: (i, 0)), pl.BlockSpec((1, d), lambda i: (0, 0))],
        out_specs=pl.BlockSpec((tm, d), lambda i: (i, 0)),
        out_shape=jax.ShapeDtypeStruct((m, d), out_dtype),
        name="rmsnorm_rows",
        compiler_params=_params("parallel"),
    )(x, g.reshape(1, d))


def _mm_kernel(*refs, nk, relu2, has_res, use_scratch):
    a_ref, w_ref = refs[0], refs[1]
    r_ref = refs[2] if has_res else None
    o_ref = refs[3] if has_res else refs[2]
    def part():
        return jnp.dot(a_ref[...], w_ref[...], preferred_element_type=F32)

    def finish(acc):
        if relu2:
            acc = jnp.square(jnp.maximum(acc, 0.0))
        return acc.astype(o_ref.dtype)

    if nk == 1:
        o_ref[...] = finish(r_ref[...] + part() if has_res else part())
        return
    acc_ref = refs[-1] if use_scratch else o_ref
    k = pl.program_id(2)

    @pl.when(k == 0)
    def _():
        acc_ref[...] = (r_ref[...] + part()) if has_res else part()

    @pl.when(k > 0)
    def _():
        acc_ref[...] = acc_ref[...] + part()

    if use_scratch or relu2:
        @pl.when(k == nk - 1)
        def _():
            o_ref[...] = finish(acc_ref[...])


def matmul(a, w, layer, *, tm, tn, tk=None, residual=None, relu2=False, out_dtype=F32, n_cols=None, col_block=0):
    m, kd = a.shape
    n = w.shape[2] if n_cols is None else n_cols
    tm = min(tm, m)
    tk = kd if tk is None else tk
    nk = kd // tk
    has_res = residual is not None
    use_scratch = nk > 1 and out_dtype != F32
    in_specs = [pl.BlockSpec((tm, tk), lambda i, j, k: (i, k)),
                pl.BlockSpec((None, tk, tn), lambda i, j, k: (layer, k, j + col_block))]
    args = [a, w]
    if has_res:
        in_specs.append(pl.BlockSpec((tm, tn), lambda i, j, k: (i, j)))
        args.append(residual)
    return pl.pallas_call(
        functools.partial(_mm_kernel, nk=nk, relu2=relu2, has_res=has_res, use_scratch=use_scratch),
        grid=(m // tm, n // tn, nk),
        in_specs=in_specs,
        out_specs=pl.BlockSpec((tm, tn), lambda i, j, k: (i, j)),
        out_shape=jax.ShapeDtypeStruct((m, n), out_dtype),
        scratch_shapes=[pltpu.VMEM((tm, tn), F32)] if use_scratch else [],
        name="matmul",
        compiler_params=_params("parallel", "parallel", "arbitrary"),
    )(*args)


def _rope_tables(pos_list, theta, width_each):
    half = width_each // 2
    inv = theta ** (-jnp.arange(half, dtype=F32) / half)
    cos_parts, sin_parts = [], []
    for pos in pos_list:
        ang = pos[:, None] * inv[None, :]
        c, s = jnp.cos(ang), jnp.sin(ang)
        cos_parts += [c, c]
        sin_parts += [-s, s]
    length = pos_list[0].shape[0]
    rest = HEAD_DIM - width_each * len(pos_list)
    if rest:
        cos_parts.append(jnp.ones((length, rest), F32))
        sin_parts.append(jnp.zeros((length, rest), F32))
    return jnp.concatenate(cos_parts, axis=1), jnp.concatenate(sin_parts, axis=1)


def _head_prep_kernel(*refs, groups, tm, half, use_norm, use_rope, scale, ones_pad):
    x_ref = refs[0]
    n_out = len(groups)
    o_refs = refs[-n_out - 1:-1]
    buf_ref = refs[-1]
    idx = 1
    if use_norm:
        g = refs[idx][...]
        idx += 1
    if use_rope:
        cos = refs[idx][...]
        sin = refs[idx + 1][...]
        src = lax.broadcasted_iota(jnp.int32, (2 * HEAD_DIM, HEAD_DIM), 0) % HEAD_DIM
        dst = lax.broadcasted_iota(jnp.int32, (2 * HEAD_DIM, HEAD_DIM), 1)
        partner_of_dst = jnp.where((dst % (2 * half)) < half, dst + half, dst - half)
        swap = (src == partner_of_dst).astype(BF16)
    h = 0
    for o_ref, (n_heads, dilation) in zip(o_refs, groups):
        for hg in range(n_heads):
            x = x_ref[:, h * HEAD_DIM:(h + 1) * HEAD_DIM]
            if use_norm:
                ms = jnp.mean(x * x, axis=-1, keepdims=True)
                x = x * lax.rsqrt(ms + EPS) * g
            if use_rope:
                hi = x.astype(BF16)
                lo = (x - hi.astype(F32)).astype(BF16)
                partner = jnp.dot(jnp.concatenate([hi, lo], axis=1), swap, preferred_element_type=F32)
                x = x * cos + partner * sin
            if scale != 1.0:
                x = x * scale
            if ones_pad:
                o_ref[hg, :, :HEAD_DIM] = x.astype(o_ref.dtype)
                o_ref[hg, :, HEAD_DIM:] = jnp.ones((tm, HEAD_DIM), o_ref.dtype)
            elif dilation == 1:
                o_ref[hg] = x.astype(o_ref.dtype)
            else:
                buf_ref[h] = x
                for r in range(dilation):
                    o_ref[hg, :, r * HEAD_DIM:(r + 1) * HEAD_DIM] = buf_ref[
                        h, pl.ds(r, tm // dilation, stride=dilation), :].astype(o_ref.dtype)
            h += 1


def head_prep(proj, col_off, groups, *, gain=None, tables=None, half=None, scale=1.0, ones_pad=False, tm=512):
    length = proj.shape[0]
    n_heads = sum(n for n, _ in groups)
    width = n_heads * HEAD_DIM
    in_specs = [pl.BlockSpec((tm, width), lambda i: (i, col_off // width))]
    args = [proj]
    if gain is not None:
        in_specs.append(pl.BlockSpec((1, HEAD_DIM), lambda i: (0, 0)))
        args.append(gain.reshape(1, HEAD_DIM))
    if tables is not None:
        in_specs += [pl.BlockSpec((tm, HEAD_DIM), lambda i: (i, 0))] * 2
        args += list(tables)
    lanes = 2 if ones_pad else 1
    outs = pl.pallas_call(
        functools.partial(_head_prep_kernel, groups=groups, tm=tm, half=half, use_norm=gain is not None,
                          use_rope=tables is not None, scale=scale, ones_pad=ones_pad),
        grid=(length // tm,),
        in_specs=in_specs,
        out_specs=[pl.BlockSpec((n, tm // d, lanes * d * HEAD_DIM), lambda i: (0, i, 0)) for n, d in groups],
        out_shape=[jax.ShapeDtypeStruct((n, length // d, lanes * d * HEAD_DIM), BF16) for n, d in groups],
        scratch_shapes=[pltpu.VMEM((n_heads, tm, HEAD_DIM), F32)],
        name="head_prep",
        compiler_params=_params("parallel"),
    )(*args)
    return outs


def _flash_kernel(q_ref, k_ref, v_ref, o_ref, m_ref, acc_ref, *, tq, nkv, chunk, nchunk):
    ki = pl.program_id(2)

    @pl.when(ki == 0)
    def _():
        m_ref[...] = jnp.full(m_ref.shape, NEG_BIG, F32)
        acc_ref[...] = jnp.zeros(acc_ref.shape, F32)

    q = q_ref[...].reshape(GQA_G * tq, HEAD_DIM)
    ntile = chunk // HEAD_DIM

    def scores(j):
        return lax.dot_general(q, k_ref[0, j * chunk:(j + 1) * chunk, :], (((1,), (1,)), ((), ())),
                               preferred_element_type=F32)

    heads = range(GQA_G)
    rows = [slice(g * tq, (g + 1) * tq) for g in heads]
    m = [m_ref[r, :] for r in rows]
    acc = [acc_ref[r, :] for r in rows]
    s_next = scores(0)
    for j in range(nchunk):
        s = s_next
        if j + 1 < nchunk:
            s_next = scores(j + 1)
        alphas, p_rows = [], []
        for g in heads:
            tiles = [s[rows[g], t * HEAD_DIM:(t + 1) * HEAD_DIM] for t in range(ntile)]
            part = tiles[0]
            for t in tiles[1:]:
                part = jnp.maximum(part, t)
            m_new = jnp.maximum(m[g], jnp.max(part, axis=-1, keepdims=True))
            alpha = jnp.exp2(m[g] - m_new)
            alphas.append(jnp.concatenate([alpha, alpha], axis=1))
            p_rows.append(jnp.concatenate([jnp.exp2(t - m_new).astype(BF16) for t in tiles], axis=1))
            m[g] = m_new
        pv = jnp.dot(jnp.concatenate(p_rows, axis=0), v_ref[0, j * chunk:(j + 1) * chunk, :],
                     preferred_element_type=F32)
        for g in heads:
            acc[g] = alphas[g] * acc[g] + pv[rows[g]]
    for g in heads:
        m_ref[rows[g], :], acc_ref[rows[g], :] = m[g], acc[g]

    @pl.when(ki == nkv - 1)
    def _():
        for g in heads:
            o = acc[g][:, :HEAD_DIM] / acc[g][:, HEAD_DIM:]
            o_ref[:, g * HEAD_DIM:(g + 1) * HEAD_DIM] = o.astype(o_ref.dtype)


def axial_flash_attention(q, k, v, *, tq=256, tk=16384, chunk=512):
    length = q.shape[1]
    tk = min(tk, length)
    chunk = min(chunk, tk)
    nkv = length // tk
    rows = GQA_G * tq
    return pl.pallas_call(
        functools.partial(_flash_kernel, tq=tq, nkv=nkv, chunk=chunk, nchunk=tk // chunk),
        grid=(KV_C, length // tq, nkv),
        in_specs=[pl.BlockSpec((GQA_G, tq, HEAD_DIM), lambda h, i, j: (h, i, 0)),
                  pl.BlockSpec((1, tk, HEAD_DIM), lambda h, i, j: (h, j, 0)),
                  pl.BlockSpec((1, tk, 2 * HEAD_DIM), lambda h, i, j: (h, j, 0))],
        out_specs=pl.BlockSpec((tq, GQA_G * HEAD_DIM), lambda h, i, j: (i, h)),
        out_shape=jax.ShapeDtypeStruct((length, H_C * HEAD_DIM), BF16),
        scratch_shapes=[pltpu.VMEM((rows, HEAD_DIM), F32), pltpu.VMEM((rows, 2 * HEAD_DIM), F32)],
        name="axial_flash",
        compiler_params=_params("parallel", "parallel", "arbitrary"),
    )(q, k, v)


def _dilated_kernel(q_ref, kp_ref, kc_ref, kn_ref, vp_ref, vc_ref, vn_ref, o_ref, lse_ref, *, tq, side, n_sub):
    c = pl.program_id(1)
    row = lax.broadcasted_iota(jnp.int32, (tq, tq + 2 * side), 0)
    col = lax.broadcasted_iota(jnp.int32, (tq, tq + 2 * side), 1)
    key_idx = c * tq - side + col
    valid = (jnp.abs(row + side - col) <= side) & (key_idx >= 0) & (key_idx < n_sub)
    heads = range(H_B_GROUP)
    halo = lambda p_ref, c_ref, n_ref, h: jnp.concatenate([p_ref[h, tq - side:, :], c_ref[h], n_ref[h, :side, :]], axis=0)
    ss = [lax.dot_general(q_ref[h], halo(kp_ref, kc_ref, kn_ref, h), (((1,), (1,)), ((), ())),
                          preferred_element_type=F32) for h in heads]
    ss = [jnp.where(valid, s, NEG_BIG) for s in ss]
    ms = [jnp.max(s, axis=-1, keepdims=True) for s in ss]
    es = [jnp.exp(s - m) for s, m in zip(ss, ms)]
    dens = [jnp.sum(e, axis=-1, keepdims=True) for e in es]
    os_ = [jnp.dot(e.astype(BF16), halo(vp_ref, vc_ref, vn_ref, h), preferred_element_type=F32)
           for e, h in zip(es, heads)]
    for h in heads:
        o_ref[h] = os_[h] / dens[h]
        lse_ref[h] = jnp.broadcast_to(ms[h] + jnp.log(dens[h]), (tq, HEAD_DIM))


def dilated_group_attention(q, k, v, window, dilation, *, tq=256):
    d = dilation
    n_sub = q.shape[1]
    side = (window // 2) // d
    nblk = n_sub // tq
    cur = lambda r, c: (0, c, r)
    prev = lambda r, c: (0, jnp.maximum(c - 1, 0), r)
    nxt = lambda r, c: (0, jnp.minimum(c + 1, nblk - 1), r)
    blk = (H_B_GROUP, tq, HEAD_DIM)
    return pl.pallas_call(
        functools.partial(_dilated_kernel, tq=tq, side=side, n_sub=n_sub),
        grid=(d, nblk),
        in_specs=[pl.BlockSpec(blk, cur), pl.BlockSpec(blk, prev), pl.BlockSpec(blk, cur), pl.BlockSpec(blk, nxt),
                  pl.BlockSpec(blk, prev), pl.BlockSpec(blk, cur), pl.BlockSpec(blk, nxt)],
        out_specs=[pl.BlockSpec(blk, cur)] * 2,
        out_shape=[jax.ShapeDtypeStruct((H_B_GROUP, n_sub, d * HEAD_DIM), F32)] * 2,
        name="dilated_attn",
        compiler_params=_params("parallel", "parallel"),
    )(q, k, k, k, v, v, v)


def _conv_prep_kernel(cur_ref, prev_ref, next_ref, w_ref, o_ref, *, tm, nblk):
    i = pl.program_id(0)
    j = pl.program_id(1)
    x = cur_ref[...]
    before = jnp.where(i > 0, prev_ref[7:8, :], 0.0)
    after = jnp.where(i < nblk - 1, next_ref[0:2, :], 0.0)
    row = lax.broadcasted_iota(jnp.int32, x.shape, 0)
    xm1 = jnp.where(row == 0, before, pltpu.roll(x, 1, axis=0))
    xp1 = jnp.where(row == tm - 1, after[0:1, :], pltpu.roll(x, tm - 1, axis=0))
    xp2 = jnp.where(row == tm - 2, after[0:1, :],
                    jnp.where(row == tm - 1, after[1:2, :], pltpu.roll(x, tm - 2, axis=0)))
    w = w_ref[...]
    y = xm1 * w[0:1, :] + x * w[1:2, :] + xp1 * w[2:3, :] + xp2 * w[3:4, :]
    y = y * jax.nn.sigmoid(y)
    post = jnp.where(j == 0, SCALE, 1.0)
    for h in range(H_A):
        yh = y[:, h * HEAD_DIM:(h + 1) * HEAD_DIM]
        inv = lax.rsqrt(jnp.sum(yh * yh, axis=-1, keepdims=True) + EPS) * post
        o_ref[:, h * HEAD_DIM:(h + 1) * HEAD_DIM] = yh * jnp.where(j < 2, inv, 1.0)


def deltanet_conv_prep(proj, conv_w, *, tm=256):
    length = proj.shape[0]
    nblk = length // tm
    cb = 0
    hb = tm // 8
    return pl.pallas_call(
        functools.partial(_conv_prep_kernel, tm=tm, nblk=nblk),
        grid=(nblk, 3),
        in_specs=[pl.BlockSpec((tm, W_A), lambda i, j: (i, cb + j)),
                  pl.BlockSpec((8, W_A), lambda i, j: (jnp.maximum(i * hb - 1, 0), cb + j)),
                  pl.BlockSpec((8, W_A), lambda i, j: (jnp.minimum((i + 1) * hb, nblk * hb - 1), cb + j)),
                  pl.BlockSpec((CONV_W, W_A), lambda i, j: (0, j))],
        out_specs=pl.BlockSpec((tm, W_A), lambda i, j: (i, j)),
        out_shape=jax.ShapeDtypeStruct((length, 3 * W_A), F32),
        name="delta_conv_prep",
        compiler_params=_params("parallel", "parallel"),
    )(proj, proj, proj, conv_w)


def _gate_prep_kernel(x_ref, a_ref, dt_ref, o_ref):
    x = x_ref[...]
    lane = lax.broadcasted_iota(jnp.int32, x.shape, 1)
    beta = jax.nn.sigmoid(x)
    z = x + dt_ref[...]
    softplus = jnp.maximum(z, 0.0) + jnp.log(1.0 + jnp.exp(-jnp.abs(z)))
    g = -jnp.exp(a_ref[...]) * softplus
    o_ref[...] = jnp.where(lane < 2 * H_A, beta, jnp.where(lane < 4 * H_A, g, 0.0))


def deltanet_gate_prep(proj, a_log, dt_bias, *, tm=512):
    length = proj.shape[0]
    pad = lambda t: jnp.zeros((1, HEAD_DIM), F32).at[0, 2 * H_A:4 * H_A].set(t.reshape(-1))
    return pl.pallas_call(
        _gate_prep_kernel,
        grid=(length // tm,),
        in_specs=[pl.BlockSpec((tm, HEAD_DIM), lambda i: (i, 0)),
                  pl.BlockSpec((1, HEAD_DIM), lambda i: (0, 0)),
                  pl.BlockSpec((1, HEAD_DIM), lambda i: (0, 0))],
        out_specs=pl.BlockSpec((tm, HEAD_DIM), lambda i: (i, 0)),
        out_shape=jax.ShapeDtypeStruct((length, HEAD_DIM), F32),
        name="delta_gate_prep",
        compiler_params=_params("parallel"),
    )(proj, pad(a_log), pad(dt_bias))


def _bdot(a, b):
    return jnp.dot(a.astype(BF16), b.astype(BF16), preferred_element_type=F32)


def _split_dot(tri, x):
    hi = x.astype(BF16)
    lo = (x - hi.astype(F32)).astype(BF16)
    return jnp.dot(tri, hi, preferred_element_type=F32) + jnp.dot(tri, lo, preferred_element_type=F32)


def _delta_kernel(qf, kf, vf, gf, qb, kb, vb, gbk, of_ref, ob_ref, s_ref, *, chunk):
    @pl.when(pl.program_id(0) == 0)
    def _():
        s_ref[...] = jnp.zeros(s_ref.shape, F32)

    cc = chunk
    row = lax.broadcasted_iota(jnp.int32, (cc, cc), 0)
    col = lax.broadcasted_iota(jnp.int32, (cc, cc), 1)
    eye = (row == col).astype(F32)
    level_masks = []
    differ = row ^ col
    b = 2
    while b <= cc:
        level_masks.append((differ < b) & (differ >= b // 2))
        b *= 2

    chains = []
    for direction, (q_ref, k_ref, v_ref, gb_ref, o_ref) in enumerate(
            ((qf, kf, vf, gf, of_ref), (qb, kb, vb, gbk, ob_ref))):
        incl = (row >= col) if direction == 0 else (row <= col)
        strict = (row > col) if direction == 0 else (row < col)
        last = cc - 1 if direction == 0 else 0
        gb = gb_ref[...]
        gc_all = _split_dot(incl.astype(BF16), gb)
        gc_rows = jnp.concatenate([gc_all, jnp.zeros_like(gc_all)], axis=0).T if cc < HEAD_DIM else gc_all.T
        for h in range(H_A):
            lane_beta = direction * H_A + h
            lane_g = 2 * H_A + direction * H_A + h
            sl = slice(h * HEAD_DIM, (h + 1) * HEAD_DIM)
            chains.append(dict(
                q=q_ref[:, sl], k=k_ref[:, sl], v=v_ref[:, sl], o_ref=o_ref, sl=sl, idx=direction * H_A + h,
                beta=gb[:, lane_beta:lane_beta + 1],
                gcc=gc_all[:, lane_g:lane_g + 1],
                gcr=gc_rows[lane_g:lane_g + 1, 0:cc],
                incl=incl, strict=strict, last=last))

    for c in chains:
        c["decay"] = jnp.exp(jnp.where(c["incl"], c["gcc"] - c["gcr"], NEG_BIG))
        c["egc"] = jnp.exp(c["gcc"])
        c["kb"] = c["k"] * c["beta"]
        c["vb"] = c["v"] * c["beta"]
    nt = (((1,), (1,)), ((), ()))
    kks = [lax.dot_general(jnp.concatenate([c["kb"], c["q"]], axis=0).astype(BF16), c["k"].astype(BF16), nt,
                           preferred_element_type=F32) for c in chains]
    for c, kk in zip(chains, kks):
        c["a"] = jnp.where(c["strict"], kk[:cc] * c["decay"], 0.0)
        c["qk"] = kk[cc:] * c["decay"]
    ts = [eye - jnp.where(level_masks[0], c["a"], 0.0) for c in chains]
    for mask in level_masks[1:]:
        tb = [_bdot(t, jnp.where(mask, c["a"], 0.0)) for t, c in zip(ts, chains)]
        tbt = [_bdot(x, t) for x, t in zip(tb, ts)]
        ts = [t - x for t, x in zip(ts, tbt)]
    uws = [_bdot(t, jnp.concatenate([c["vb"], c["kb"] * c["egc"]], axis=1)) for t, c in zip(ts, chains)]
    s_olds = [s_ref[c["idx"]] for c in chains]
    wqs = [_bdot(jnp.concatenate([uw[:, HEAD_DIM:], c["q"] * c["egc"]], axis=0), s_old)
           for uw, c, s_old in zip(uws, chains, s_olds)]
    v_news = [uw[:, :HEAD_DIM] - wq[:cc] for uw, wq in zip(uws, wqs)]
    intra = [_bdot(c["qk"], v_new) for c, v_new in zip(chains, v_news)]
    for c, wq, x in zip(chains, wqs, intra):
        c["o_ref"][:, c["sl"]] = wq[cc:] + x
    tn = (((0,), (0,)), ((), ()))
    g_lasts = [c["gcc"][c["last"]:c["last"] + 1, :] for c in chains]
    kvs = [lax.dot_general((c["k"] * jnp.exp(g_last - c["gcc"])).astype(BF16), v_new.astype(BF16), tn,
                           preferred_element_type=F32) for c, g_last, v_new in zip(chains, g_lasts, v_news)]
    for c, s_old, g_last, kv in zip(chains, s_olds, g_lasts, kvs):
        s_ref[c["idx"]] = s_old * jnp.exp(g_last) + kv


def deltanet_recurrence(qkv, gb, *, chunk=DN_CHUNK):
    length = qkv.shape[0]
    n = length // chunk
    fwd = lambda j: (lambda c: (c, j))
    bwd = lambda j: (lambda c: (n - 1 - c, j))
    col = lambda m, width: pl.BlockSpec((chunk, width), m)
    return pl.pallas_call(
        functools.partial(_delta_kernel, chunk=chunk),
        grid=(n,),
        in_specs=[col(fwd(0), W_A), col(fwd(1), W_A), col(fwd(2), W_A), col(fwd(0), HEAD_DIM),
                  col(bwd(0), W_A), col(bwd(1), W_A), col(bwd(2), W_A), col(bwd(0), HEAD_DIM)],
        out_specs=[col(fwd(0), W_A), col(bwd(0), W_A)],
        out_shape=[jax.ShapeDtypeStruct((length, W_A), F32)] * 2,
        scratch_shapes=[pltpu.VMEM((2 * H_A, HEAD_DIM, HEAD_DIM), F32)],
        name="delta_recurrence",
        compiler_params=_params("arbitrary"),
    )(qkv, qkv, qkv, gb, qkv, qkv, qkv, gb)


def _assemble_kernel(of_ref, ob_ref, gate_ref, og_ref, o0, o1, o2, l0, l1, l2, yc_ref, mix_ref, buf_ref, *, tm):
    og = og_ref[...]
    for h in range(H_A):
        sl = slice(h * HEAD_DIM, (h + 1) * HEAD_DIM)
        o = of_ref[:, sl] + ob_ref[:, sl]
        ms = jnp.mean(o * o, axis=-1, keepdims=True)
        gate = gate_ref[:, sl]
        mix_ref[:, sl] = (o * lax.rsqrt(ms + EPS) * og * (gate * jax.nn.sigmoid(gate))).astype(mix_ref.dtype)

    def positions(ref, h, gi, slot):
        d = DILATED_PAIRS[gi][1]
        if d == 1:
            return ref[h]
        for r in range(d):
            buf_ref[slot, pl.ds(r, tm // d, stride=d), :] = ref[h, :, r * HEAD_DIM:(r + 1) * HEAD_DIM]
        return buf_ref[slot]

    outs, lses = (o0, o1, o2), (l0, l1, l2)
    n_groups = len(DILATED_PAIRS)
    for h in range(H_B_GROUP):
        ls = [positions(lses[gi], h, gi, (h * n_groups + gi) * 2) for gi in range(n_groups)]
        mx = jnp.maximum(jnp.maximum(ls[0], ls[1]), ls[2])
        es = [jnp.exp(t - mx) for t in ls]
        tot = es[0] + es[1] + es[2]
        for gi in range(n_groups):
            c0 = W_A + (gi * H_B_GROUP + h) * HEAD_DIM
            o = positions(outs[gi], h, gi, (h * n_groups + gi) * 2 + 1)
            mix_ref[:, c0:c0 + HEAD_DIM] = (o * (es[gi] / tot)).astype(mix_ref.dtype)
    mix_ref[:, W_A + W_B:] = yc_ref[...]


def assemble_mix(o_fwd, o_bwd, proj, onorm_g, dil_outs, dil_lses, y_c, *, tm=256):
    length = o_fwd.shape[0]
    row = lambda w: pl.BlockSpec((tm, w), lambda i: (i, 0))
    hblks = [pl.BlockSpec((H_B_GROUP, tm // d, d * HEAD_DIM), lambda i: (0, i, 0)) for _, d in DILATED_PAIRS]
    return pl.pallas_call(
        functools.partial(_assemble_kernel, tm=tm),
        grid=(length // tm,),
        in_specs=[row(W_A), row(W_A), pl.BlockSpec((tm, W_A), lambda i: (i, OFF_AGATE // W_A)),
                  pl.BlockSpec((1, HEAD_DIM), lambda i: (0, 0))] + hblks * 2 + [row(W_CQ)],
        out_specs=row(D_MODEL),
        out_shape=jax.ShapeDtypeStruct((length, D_MODEL), BF16),
        scratch_shapes=[pltpu.VMEM((2 * H_B, tm, HEAD_DIM), F32)],
        name="assemble_mix",
        compiler_params=_params("parallel"),
    )(o_fwd, o_bwd, proj, onorm_g.reshape(1, HEAD_DIM), *dil_outs, *dil_lses, y_c)


def _cross_kernel(x_ref, gx_ref, wq_ref, k_ref, v_ref, wo_ref, gf_ref, xo_ref, hf_ref):
    x = x_ref[...]
    ms = jnp.mean(x * x, axis=-1, keepdims=True)
    h = (x * lax.rsqrt(ms + EPS) * gx_ref[...]).astype(BF16)
    q = jnp.dot(h, wq_ref[...], preferred_element_type=F32)
    sls = [slice(hh * HEAD_DIM, (hh + 1) * HEAD_DIM) for hh in range(X_HEADS)]
    ss = [lax.dot_general(q[:, sl].astype(BF16), k_ref[:, sl], (((1,), (1,)), ((), ())),
                          preferred_element_type=F32) * SCALE for sl in sls]
    es = [jnp.exp(s - jnp.max(s, axis=-1, keepdims=True)) for s in ss]
    ps = [e / jnp.sum(e, axis=-1, keepdims=True) for e in es]
    heads = [jnp.dot(p.astype(BF16), v_ref[:, sl], preferred_element_type=F32).astype(BF16) for p, sl in zip(ps, sls)]
    o = jnp.concatenate(heads, axis=1)
    xn = x + jnp.dot(o, wo_ref[...], preferred_element_type=F32)
    xo_ref[...] = xn
    ms2 = jnp.mean(xn * xn, axis=-1, keepdims=True)
    hf_ref[...] = (xn * lax.rsqrt(ms2 + EPS) * gf_ref[...]).astype(hf_ref.dtype)


def cross_attention_block(x, g_x, wq, kv, wo, g_ffn, layer, *, tm=256):
    length = x.shape[0]
    width = X_HEADS * HEAD_DIM
    full = lambda shape: pl.BlockSpec(shape, lambda i: (0, 0))
    stacked = lambda shape: pl.BlockSpec((None,) + shape, lambda i: (layer, 0, 0))
    row = pl.BlockSpec((tm, D_MODEL), lambda i: (i, 0))
    return pl.pallas_call(
        _cross_kernel,
        grid=(length // tm,),
        in_specs=[row, full((1, D_MODEL)), stacked((D_MODEL, width)),
                  pl.BlockSpec((N_MEM, width), lambda i: (0, 0)), pl.BlockSpec((N_MEM, width), lambda i: (0, 1)),
                  stacked((width, D_MODEL)), full((1, D_MODEL))],
        out_specs=[row, row],
        out_shape=[jax.ShapeDtypeStruct((length, D_MODEL), F32), jax.ShapeDtypeStruct((length, D_MODEL), BF16)],
        name="cross_attn",
        compiler_params=_params("parallel"),
    )(x, g_x.reshape(1, D_MODEL), wq, kv, kv, wo, g_ffn.reshape(1, D_MODEL))


def _split_w_in(w):
    w_all = w.astype(BF16)
    w_bc = w_all[..., COLS_A + COLS_S:]
    return w_all, w_bc


def _encode(x, mem, tables_b, tables_c, p):
    dil_groups = tuple((H_B_GROUP, d) for _, d in DILATED_PAIRS)
    for l in range(p["w_out"].shape[0]):
        h = rmsnorm_rows(x, p["norm_mix_g"][l], BF16)
        proj_a = matmul(h, p["w_in"], l, tm=1024, tn=1024, n_cols=COLS_A)
        proj_s = matmul(h, p["w_in"], l, tm=1024, tn=HEAD_DIM, n_cols=HEAD_DIM, col_block=COLS_A // HEAD_DIM)
        proj_bc = matmul(h, p["w_in_bc"], l, tm=1024, tn=1024)
        qkv_a = deltanet_conv_prep(proj_a, p["conv_w"][l])
        gb = deltanet_gate_prep(proj_s, p["a_log"][l], p["dt_bias"][l])
        o_fwd, o_bwd = deltanet_recurrence(qkv_a, gb)
        q_b = head_prep(proj_bc, OFF_BQ, dil_groups, tables=tables_b, half=ROT_DIMS // 2, scale=SCALE)
        k_b = head_prep(proj_bc, OFF_BK, dil_groups, tables=tables_b, half=ROT_DIMS // 2)
        v_b = head_prep(proj_bc, OFF_BV, dil_groups)
        dil = [dilated_group_attention(q_b[gi], k_b[gi], v_b[gi], w, d) for gi, (w, d) in enumerate(DILATED_PAIRS)]
        q_c, = head_prep(proj_bc, OFF_CQ, ((H_C, 1),), gain=p["qnorm_g"][l], tables=tables_c, half=HEAD_DIM // 4,
                         scale=SCALE * LOG2_E)
        k_c, = head_prep(proj_bc, OFF_CK, ((KV_C, 1),), gain=p["knorm_g"][l], tables=tables_c, half=HEAD_DIM // 4)
        v_c, = head_prep(proj_bc, OFF_CV, ((KV_C, 1),), ones_pad=True)
        y_c = axial_flash_attention(q_c, k_c, v_c)
        mix = assemble_mix(o_fwd, o_bwd, proj_a, p["onorm_g"][l], [t[0] for t in dil], [t[1] for t in dil], y_c)
        x = matmul(mix, p["w_out"], l, tm=1024, tn=1024, residual=x)
        mem_n = rmsnorm_rows(mem, p["norm_mem_g"][l], BF16)
        kv = matmul(mem_n, p["wkv_x"], l, tm=N_MEM, tn=512, out_dtype=BF16)
        x, hf = cross_attention_block(x, p["norm_x_g"][l], p["wq_x"], kv, p["wo_x"], p["norm_ffn_g"][l], l)
        up = matmul(hf, p["w_up"], l, tm=1024, tn=1024, relu2=True, out_dtype=BF16)
        x = matmul(up, p["w_down"], l, tm=1024, tn=1024, tk=2048, residual=x)
    return rmsnorm_rows(x, p["norm_final_g"], F32)


def _tables(length):
    pos = jnp.arange(length, dtype=F32)
    tables_b = _rope_tables([pos], ROPE_THETA, ROT_DIMS)
    rows = length // GRID_W
    row_pos = jnp.repeat(jnp.arange(rows, dtype=F32), GRID_W)
    col_pos = jnp.tile(jnp.arange(GRID_W, dtype=F32), rows)
    tables_c = _rope_tables([row_pos, col_pos], AXIAL_THETA, HEAD_DIM // 2)
    return tables_b, tables_c


def kernel(x_prompt, x_sample, mem_prompt, mem_sample, norm_mix_g, w_in, conv_w, a_log, dt_bias, onorm_g, qnorm_g, knorm_g, w_out, norm_x_g, norm_mem_g, wq_x, wk_x, wv_x, wo_x, norm_ffn_g, w_up, w_down, norm_final_g):
    w_in_all, w_in_bc = _split_w_in(w_in)
    p = dict(
        norm_mix_g=norm_mix_g, conv_w=conv_w, a_log=a_log, dt_bias=dt_bias, onorm_g=onorm_g, qnorm_g=qnorm_g,
        knorm_g=knorm_g, norm_x_g=norm_x_g, norm_mem_g=norm_mem_g, norm_ffn_g=norm_ffn_g, norm_final_g=norm_final_g,
        w_in=w_in_all, w_in_bc=w_in_bc,
        w_out=w_out.astype(BF16), wq_x=wq_x.astype(BF16),
        wkv_x=jnp.concatenate([wk_x, wv_x], axis=-1).astype(BF16),
        wo_x=wo_x.astype(BF16), w_up=w_up.astype(BF16), w_down=w_down.astype(BF16),
    )
    outs = []
    for x, mem in ((x_prompt, mem_prompt), (x_sample, mem_sample)):
        length = x.shape[1]
        tables_b, tables_c = _tables(length)
        y = _encode(x[0], mem[0], tables_b, tables_c, p)
        outs.append(y[None])
    return tuple(outs)
```
